```python
import jax
import jax.numpy as jnp
from jax import lax
import numpy as np

D_MODEL = 1024
BATCH = 8
SEQ = 4096
DEPTH = 4

MIX_WIDTH = 512
FOX_HEAD_DIM = 64
FOX_HEADS = MIX_WIDTH // FOX_HEAD_DIM
FOX_BLOCK = 128
HGRN_KEY_DIM = 64
HGRN_HEADS = MIX_WIDTH // HGRN_KEY_DIM
HGRN_VAL_DIM = MIX_WIDTH // HGRN_HEADS
HGRN_CHUNK = 64
RWKV_HEAD_DIM = 64
RWKV_HEADS = MIX_WIDTH // RWKV_HEAD_DIM
RWKV_DECAY_RANK = 64
RWKV_ICLR_RANK = 64
RWKV_VRES_RANK = 32
RWKV_GATE_RANK = 128
RWKV_LN_EPS = 64e-5
N_BRANCH = 3
D_FF = 2816
CONV_WIDTH = 3
NORM_EPS = 1e-6
MASK_VALUE = -1e30
LOG_FLOOR = 1e-30

FOX_SIZES = (MIX_WIDTH, MIX_WIDTH, MIX_WIDTH, FOX_HEADS)
HGRN_SIZES = (MIX_WIDTH, MIX_WIDTH, MIX_WIDTH, MIX_WIDTH)
RWKV_SIZES = (MIX_WIDTH, MIX_WIDTH, MIX_WIDTH, RWKV_DECAY_RANK, RWKV_ICLR_RANK, RWKV_GATE_RANK)
GATE_SIZES = (D_MODEL, D_MODEL, D_MODEL)
GROUP_SIZES = (sum(FOX_SIZES), sum(HGRN_SIZES), sum(RWKV_SIZES), sum(GATE_SIZES))
N_IN = sum(GROUP_SIZES)

kernel_name = 'hybrid_fox_hgrn2_rwkv7_block'


def split_cols(p, sizes):
    return jnp.split(p, np.cumsum(sizes)[:-1].tolist(), axis=-1)


def rms_norm(x, eps=NORM_EPS):
    xf = x.astype(jnp.float32)
    return (xf * lax.rsqrt(jnp.mean(xf * xf, axis=-1, keepdims=True) + eps)).astype(x.dtype)


def token_shift(p):
    return jnp.pad(p, ((0, 0), (1, 0), (0, 0)))[:, :-1]


def split_heads(t, n_heads):
    return t.reshape(t.shape[0], t.shape[1], n_heads, t.shape[2] // n_heads)


def merge_heads(t):
    return t.reshape(t.shape[0], t.shape[1], -1)


def causal_dwconv(u, w, b):
    k_w, s_len = w.shape[0], u.shape[1]
    up = jnp.pad(u, ((0, 0), (k_w - 1, 0), (0, 0)))
    out = b
    for j in range(k_w):
        out = out + up[:, j:j + s_len] * w[j]
    return out


def fox_attention(q, k, v, log_f, q_gain, k_gain):
    s_len, dh = q.shape[1], q.shape[3]
    q = rms_norm(q) * q_gain
    k = rms_norm(k) * k_gain
    cum = jnp.cumsum(log_f, axis=1).transpose(0, 2, 1)
    scale = dh ** -0.5
    outs = []
    for blk in range(s_len // FOX_BLOCK):
        lo, hi = blk * FOX_BLOCK, (blk + 1) * FOX_BLOCK
        logits = jnp.einsum('bqhd,bkhd->bhqk', q[:, lo:hi], k[:, :hi]).astype(jnp.float32) * scale
        logits = logits + cum[:, :, lo:hi, None] - cum[:, :, None, :hi]
        causal = (lo + jnp.arange(FOX_BLOCK))[:, None] >= jnp.arange(hi)[None, :]
        probs = jax.nn.softmax(jnp.where(causal, logits, MASK_VALUE), axis=-1).astype(v.dtype)
        outs.append(jnp.einsum('bhqk,bkhd->bqhd', probs, v[:, :hi]))
    return jnp.concatenate(outs, axis=1)


def hgrn2_chunked(q, k, v, log_g):
    f32 = jnp.float32
    bsz, s_len, n_h, d_k = q.shape
    d_v = v.shape[-1]
    n_c = s_len // HGRN_CHUNK

    def to_chunks(t):
        return t.astype(f32).reshape(bsz, n_c, HGRN_CHUNK, n_h, t.shape[-1]).transpose(1, 0, 3, 2, 4)

    mask = jnp.tril(jnp.ones((HGRN_CHUNK, HGRN_CHUNK), dtype=bool))[:, :, None]

    def step(state, inp):
        qb, kb, vb, gb = inp
        g_cum = jnp.cumsum(gb, axis=2)
        diff = g_cum[:, :, :, None, :] - g_cum[:, :, None, :, :]
        decay = jnp.where(mask, jnp.exp(jnp.where(mask, diff, 0.0)), 0.0)
        attn = jnp.einsum('bhtk,bhsk,bhtsk->bhts', qb, kb, decay)
        o = (jnp.einsum('bhts,bhsv->bhtv', attn, vb)
             + jnp.einsum('bhtk,bhkv->bhtv', qb * jnp.exp(g_cum), state))
        g_last = g_cum[:, :, -1]
        state = (state * jnp.exp(g_last)[..., None]
                 + jnp.einsum('bhsk,bhsv->bhkv', kb * jnp.exp(g_last[:, :, None, :] - g_cum), vb))
        return state, o

    state0 = jnp.zeros((bsz, n_h, d_k, d_v), f32)
    _, o = lax.scan(step, state0, (to_chunks(q), to_chunks(k), to_chunks(v), to_chunks(log_g)))
    return o.transpose(1, 0, 3, 2, 4).reshape(bsz, s_len, n_h, d_v)


def hgrn2_branch(p, lb, o_gain):
    q, f, i, og = split_cols(p, HGRN_SIZES)
    ff = f.astype(jnp.float32)
    gate = lb + (1.0 - lb) * jax.nn.sigmoid(ff)
    log_g = jnp.log(jnp.maximum(gate, LOG_FLOOR))
    k = 1.0 - gate
    o = hgrn2_chunked(split_heads(jax.nn.silu(q), HGRN_HEADS), split_heads(k, HGRN_HEADS),
                      split_heads(i, HGRN_HEADS), split_heads(log_g, HGRN_HEADS))
    o = rms_norm(o.astype(p.dtype)) * o_gain
    return merge_heads(o) * jax.nn.silu(og)


def rwkv7_scan(r, log_w, k, v, kk, a):
    f32 = jnp.float32
    bsz, _, n_h, n = r.shape

    def step(state, inp):
        r_t, lw_t, k_t, v_t, kk_t, a_t = inp
        removed = jnp.einsum('bhvk,bhk->bhv', state, kk_t)
        state = (state * jnp.exp(lw_t)[:, :, None, :]
                 - removed[..., None] * (kk_t * a_t)[:, :, None, :]
                 + v_t[..., None] * k_t[:, :, None, :])
        return state, jnp.einsum('bhvk,bhk->bhv', state, r_t)

    seq_first = lambda t: jnp.moveaxis(t.astype(f32), 1, 0)
    state0 = jnp.zeros((bsz, n_h, n, n), f32)
    _, out = lax.scan(step, state0, tuple(seq_first(t) for t in (r, log_w, k, v, kk, a)))
    return jnp.moveaxis(out, 0, 1)


def rwkv7_branch(p, v_first, vres, mu, w0, w2, a0, a2, g2, k_k, k_a, r_k, ln_w, ln_b):
    n_h = RWKV_HEADS
    p = p + (token_shift(p) - p) * mu
    r, k, v, w_lo, a_lo, g_lo = split_cols(p, RWKV_SIZES)
    w_raw = -jax.nn.softplus(-(w0 + jnp.tanh(w_lo) @ w2)) - 0.5
    log_w = -jnp.exp(w_raw.astype(jnp.float32))
    a = jax.nn.sigmoid(a0 + a_lo @ a2)
    g = jax.nn.sigmoid(g_lo) @ g2
    if v_first is None:
        v_first = v
    else:
        vres_lo, vres_mu, v0, v2 = vres
        vres_lo = vres_lo + (token_shift(vres_lo) - vres_lo) * vres_mu
        v = v + (v_first - v) * jax.nn.sigmoid(v0 + vres_lo @ v2)
    kk = split_heads(k * k_k, n_h).astype(jnp.float32)
    kk = kk * lax.rsqrt(jnp.maximum(jnp.sum(kk * kk, axis=-1, keepdims=True), 1e-24))
    k = k * (1.0 + (a - 1.0) * k_a)
    rh, kh, vh = split_heads(r, n_h), split_heads(k, n_h), split_heads(v, n_h)
    o = rwkv7_scan(rh, split_heads(log_w, n_h), kh, vh, kk, split_heads(a, n_h))
    mean = jnp.mean(o, axis=-1, keepdims=True)
    var = jnp.mean(jnp.square(o - mean), axis=-1, keepdims=True)
    o = merge_heads((o - mean) * lax.rsqrt(var + RWKV_LN_EPS)) * ln_w + ln_b
    bonus = jnp.sum(rh * kh * r_k, axis=-1, keepdims=True) * vh
    o = o + merge_heads(bonus)
    return (o * g).astype(p.dtype), v_first


def setup_inputs(seed: int = 0) -> dict:
    key = jax.random.key(seed)
    ks = iter(jax.random.split(key, 40))
    f32 = jnp.float32

    def nrm(shape, scale):
        return jax.random.normal(next(ks), shape, f32) * scale

    def uni(shape, lo, hi):
        return jax.random.uniform(next(ks), shape, f32, lo, hi)

    L, D, W = DEPTH, D_MODEL, MIX_WIDTH
    return {
        'x': nrm((BATCH, SEQ, D), 1.0),
        'c': nrm((BATCH, D), 1.0),
        'w_ada': nrm((L, D, 6 * D), 0.5 * D ** -0.5),
        'b_ada': nrm((L, 6 * D), 0.02),
        'norm1_g': 1.0 + nrm((L, D), 0.02),
        'norm2_g': 1.0 + nrm((L, D), 0.02),
        'w_in': nrm((L, D, N_IN), D ** -0.5),
        'fox_b_f': uni((L, FOX_HEADS), 1.0, 4.0),
        'fox_q_gain': 1.0 + nrm((L, FOX_HEAD_DIM), 0.02),
        'fox_k_gain': 1.0 + nrm((L, FOX_HEAD_DIM), 0.02),
        'hgrn_lb': nrm((L, W), 1.0),
        'hgrn_o_gain': 1.0 + nrm((L, HGRN_VAL_DIM), 0.02),
        'rwkv_mu': uni((L, sum(RWKV_SIZES)), 0.0, 1.0),
        'rwkv_w0': uni((L, W), -6.0, -1.0),
        'rwkv_w2': nrm((L, RWKV_DECAY_RANK, W), 0.1 * RWKV_DECAY_RANK ** -0.5),
        'rwkv_a0': nrm((L, W), 0.1),
        'rwkv_a2': nrm((L, RWKV_ICLR_RANK, W), RWKV_ICLR_RANK ** -0.5),
        'rwkv_g2': nrm((L, RWKV_GATE_RANK, W), RWKV_GATE_RANK ** -0.5),
        'rwkv_k_k': 0.85 + nrm((L, W), 0.02),
        'rwkv_k_a': 1.0 + nrm((L, W), 0.02),
        'rwkv_r_k': nrm((L, RWKV_HEADS, RWKV_HEAD_DIM), 0.1),
        'rwkv_ln_w': 1.0 + nrm((L, W), 0.02),
        'rwkv_ln_b': nrm((L, W), 0.02),
        'rwkv_vres_down': nrm((L - 1, D, RWKV_VRES_RANK), D ** -0.5),
        'rwkv_vres_mu': uni((L - 1, RWKV_VRES_RANK), 0.0, 1.0),
        'rwkv_v0': nrm((L - 1, W), 0.1),
        'rwkv_v2': nrm((L - 1, RWKV_VRES_RANK, W), RWKV_VRES_RANK ** -0.5),
        'w_branch': nrm((L, N_BRANCH, W, D), W ** -0.5),
        'w_out': nrm((L, D, D), D ** -0.5),
        'w_up': nrm((L, D, 2 * D_FF), D ** -0.5),
        'conv_w': nrm((L, CONV_WIDTH, 2 * D_FF), CONV_WIDTH ** -0.5),
        'conv_b': nrm((L, 2 * D_FF), 0.02),
        'w_down': nrm((L, D_FF, D), D_FF ** -0.5),
    }


def reference(x, c, w_ada, b_ada, norm1_g, norm2_g, w_in, fox_b_f, fox_q_gain, fox_k_gain,
              hgrn_lb, hgrn_o_gain, rwkv_mu, rwkv_w0, rwkv_w2, rwkv_a0, rwkv_a2, rwkv_g2,
              rwkv_k_k, rwkv_k_a, rwkv_r_k, rwkv_ln_w, rwkv_ln_b, rwkv_vres_down, rwkv_vres_mu,
              rwkv_v0, rwkv_v2, w_branch, w_out, w_up, conv_w, conv_b, w_down):
    lb_prob = jax.nn.softmax(hgrn_lb.astype(jnp.float32), axis=0)
    hgrn_lower = jnp.cumsum(lb_prob, axis=0) - lb_prob[0]
    cond = jax.nn.silu(c)
    v_first = None
    for l in range(DEPTH):
        mod = cond @ w_ada[l] + b_ada[l]
        sh1, sc1, gt1, sh2, sc2, gt2 = [m[:, None, :] for m in jnp.split(mod, 6, axis=-1)]

        h = rms_norm(x) * norm1_g[l] * (1.0 + sc1) + sh1
        if l == 0:
            proj = h @ w_in[l]
            vres = None
        else:
            proj = h @ jnp.concatenate([w_in[l], rwkv_vres_down[l - 1]], axis=1)
            proj, vres_lo = proj[..., :N_IN], proj[..., N_IN:]
            vres = (vres_lo, rwkv_vres_mu[l - 1], rwkv_v0[l - 1], rwkv_v2[l - 1])
        fox_p, hgrn_p, rwkv_p, gate_p = split_cols(proj, GROUP_SIZES)

        fq, fk, fv, ff = split_cols(fox_p, FOX_SIZES)
        log_f = jax.nn.log_sigmoid((ff + fox_b_f[l]).astype(jnp.float32))
        y_fox = merge_heads(fox_attention(split_heads(fq, FOX_HEADS), split_heads(fk, FOX_HEADS),
                                          split_heads(fv, FOX_HEADS), log_f,
                                          fox_q_gain[l], fox_k_gain[l]))
        y_hgrn = hgrn2_branch(hgrn_p, hgrn_lower[l], hgrn_o_gain[l])
        y_rwkv, v_first = rwkv7_branch(rwkv_p, v_first, vres, rwkv_mu[l], rwkv_w0[l], rwkv_w2[l],
                                       rwkv_a0[l], rwkv_a2[l], rwkv_g2[l], rwkv_k_k[l], rwkv_k_a[l],
                                       rwkv_r_k[l], rwkv_ln_w[l], rwkv_ln_b[l])

        g_fox, g_hgrn, g_rwkv = [jax.nn.sigmoid(g) for g in split_cols(gate_p, GATE_SIZES)]
        merged = (g_fox * (y_fox @ w_branch[l, 0])
                  + g_hgrn * (y_hgrn @ w_branch[l, 1])
                  + g_rwkv * (y_rwkv @ w_branch[l, 2]))
        x = x + gt1 * (merged @ w_out[l])

        h = rms_norm(x) * norm2_g[l] * (1.0 + sc2) + sh2
        u = causal_dwconv(h @ w_up[l], conv_w[l], conv_b[l])
        u_val, u_gate = jnp.split(u, 2, axis=-1)
        x = x + gt2 * ((jax.nn.silu(u_gate) * u_val) @ w_down[l])
    return x
```

```python
import functools

import jax
import jax.numpy as jnp
import numpy as np
from jax import lax
from jax.experimental import pallas as pl
from jax.experimental.pallas import tpu as pltpu

F32 = jnp.float32
BF16 = jnp.bfloat16

D_MODEL = 1024
MIX = 512
HEAD = 64
N_HEADS = MIX // HEAD
N_PAIRS = N_HEADS // 2
LANES = 128
D_FF = 2816
NORM_EPS = 1e-6
RWKV_LN_EPS = 64e-5
MASK_VALUE = -1e30
LOG_FLOOR = 1e-30
VRES_RANK = 32

HGRN_SUB = 16
RWKV_CHUNK = 64

COL_GATE = 0
COL_FOX = 3072
COL_HGRN = 4608
COL_RWKV = 6144
N_MAIN = 8192
N_AUX = 640
AUX_FF_LANE = 0
AUX_VRES_LANE = 32

VMEM_LIMIT = 56 * 1024 * 1024


def _cparams(sem):
    return pltpu.CompilerParams(dimension_semantics=sem, vmem_limit_bytes=VMEM_LIMIT)


def _split3(x):
    hi = x.astype(BF16)
    r1 = x - hi.astype(F32)
    mid = r1.astype(BF16)
    lo = (r1 - mid.astype(F32)).astype(BF16)
    return hi, mid, lo


def _exact_dot(a01, x):
    hi, mid, lo = _split3(x)
    d = lambda b: jnp.dot(a01, b, preferred_element_type=F32)
    return d(hi) + d(mid) + d(lo)


def _group_sum(x, e):
    w = e.shape[0]
    xb = x.astype(BF16)
    if x.shape[1] == w:
        return jnp.dot(xb, e, preferred_element_type=F32)
    parts = [jnp.dot(xb[:, i:i + w], e, preferred_element_type=F32) for i in range(0, x.shape[1], w)]
    return jnp.concatenate(parts, axis=1)


def _group_sum2(x, e):
    hi = x.astype(BF16)
    lo = x - hi.astype(F32)
    return _group_sum(hi, e) + _group_sum(lo, e)


def _sigmoid(x):
    return 1.0 / (1.0 + jnp.exp(-x))


def _log_sigmoid(x):
    return jnp.minimum(x, 0.0) - jnp.log1p(jnp.exp(-jnp.abs(x)))


def _dot_nt(a, b):
    return lax.dot_general(a, b, (((1,), (1,)), ((), ())), preferred_element_type=F32)


def _dot_tn(a, b):
    return lax.dot_general(a, b, (((0,), (0,)), ((), ())), preferred_element_type=F32)


def _block_ones(width):
    g = np.arange(width) // HEAD
    return jnp.asarray(g[:, None] == g[None, :], dtype=BF16)


def _block_tril(n, blk):
    i = np.arange(n)
    m = (i[:, None] >= i[None, :]) & ((i[:, None] // blk) == (i[None, :] // blk))
    return jnp.asarray(m, dtype=BF16)


def _ada_kernel(c_ref, w_ref, b_ref, o_ref):
    c = c_ref[...]
    cond = c * _sigmoid(c)
    o_ref[0] = jnp.dot(cond.astype(BF16), w_ref[0].astype(BF16), preferred_element_type=F32) + b_ref[0]


def _ada_mod(c, w_ada, b_ada, tn=1536):
    depth, d, n = w_ada.shape
    bsz = c.shape[0]
    return pl.pallas_call(
        _ada_kernel,
        grid=(depth, n // tn),
        in_specs=[pl.BlockSpec((bsz, d), lambda l, j: (0, 0)),
                  pl.BlockSpec((1, d, tn), lambda l, j: (l, 0, j)),
                  pl.BlockSpec((1, 1, tn), lambda l, j: (l, 0, j))],
        out_specs=pl.BlockSpec((1, bsz, tn), lambda l, j: (l, 0, j)),
        out_shape=jax.ShapeDtypeStruct((depth, bsz, n), F32),
        compiler_params=_cparams(("parallel", "parallel")),
        name="ada_mod",
    )(c, w_ada, b_ada.reshape(depth, 1, n))


def _norm_mm_kernel(x_ref, sc_ref, sh_ref, w_ref, o_ref, h_ref):
    @pl.when(pl.program_id(1) == 0)
    def _():
        x = x_ref[...]
        ms = jnp.mean(x * x, axis=-1, keepdims=True)
        h_ref[...] = (x * lax.rsqrt(ms + NORM_EPS) * sc_ref[0] + sh_ref[0]).astype(BF16)

    o_ref[...] = jnp.dot(h_ref[...], w_ref[...], preferred_element_type=F32).astype(o_ref.dtype)


def _norm_matmul(x, scale, shift, w, out_dtype, seq, tm, tn, name):
    t, d = x.shape
    n = w.shape[1]
    tps = seq // tm
    return pl.pallas_call(
        _norm_mm_kernel,
        grid=(t // tm, n // tn),
        in_specs=[pl.BlockSpec((tm, d), lambda i, j: (i, 0)),
                  pl.BlockSpec((1, 1, d), lambda i, j: (i // tps, 0, 0)),
                  pl.BlockSpec((1, 1, d), lambda i, j: (i // tps, 0, 0)),
                  pl.BlockSpec((d, tn), lambda i, j: (0, j))],
        out_specs=pl.BlockSpec((tm, tn), lambda i, j: (i, j)),
        out_shape=jax.ShapeDtypeStruct((t, n), out_dtype),
        scratch_shapes=[pltpu.VMEM((tm, d), BF16)],
        compiler_params=_cparams(("parallel", "arbitrary")),
        name=name,
    )(x, scale, shift, w)


def _fox_prep_kernel(q_ref, k_ref, v_ref, misc_ref, qg_ref, kg_ref, bf_ref, e_ref, tri_ref,
                     qo_ref, ko_ref, vo_ref, carry_ref):
    tm = q_ref.shape[0]

    @pl.when(pl.program_id(1) == 0)
    def _():
        carry_ref[...] = jnp.zeros_like(carry_ref)

    e = e_ref[...]

    def qk_norm(ref, gain):
        xx = ref[...].astype(F32)
        ss = _group_sum(xx * xx, e)
        return xx * lax.rsqrt(ss * (1.0 / HEAD) + NORM_EPS) * gain

    qn = qk_norm(q_ref, qg_ref[...])
    kn = qk_norm(k_ref, kg_ref[...])
    vv = v_ref[...].astype(F32)

    lf = _log_sigmoid(misc_ref[...] + bf_ref[...])
    cum = _exact_dot(tri_ref[...], lf) + carry_ref[...]
    carry_ref[...] = cum[tm - 1:tm, :]
    hi, mid, lo = [s.astype(F32) for s in _split3(cum)]

    lane = lax.broadcasted_iota(jnp.int32, (tm, LANES), 1)

    def column(a, h):
        return jnp.sum(jnp.where(lane == h, a, 0.0), axis=1, keepdims=True)

    for p in range(N_PAIRS):
        sl = slice(p * LANES, (p + 1) * LANES)
        qp, kp, vp = qn[:, sl], kn[:, sl], vv[:, sl]
        for par in range(2):
            h = 2 * p + par
            fh, fm, fl = column(hi, h), column(mid, h), column(lo, h)
            base = HEAD if par == 0 else 0
            keep = (lane < HEAD) if par == 0 else (lane >= HEAD)
            is_f = (lane >= base) & (lane < base + 3)
            is_g = (lane >= base + 3) & (lane < base + 6)
            fsel = jnp.where(lane == base, fh, jnp.where(lane == base + 1, fm, fl))
            gsel = jnp.where(lane == base + 3, fh, jnp.where(lane == base + 4, fm, fl))
            ext_q = jnp.where(is_f, fsel, jnp.where(is_g, 1.0, 0.0))
            ext_k = jnp.where(is_f, 1.0, jnp.where(is_g, -gsel, 0.0))
            qo_ref[0, h] = jnp.where(keep, qp, ext_q).astype(BF16)
            ko_ref[0, h] = jnp.where(keep, kp, ext_k).astype(BF16)
            vo_ref[0, h] = jnp.where(keep, vp, jnp.where(lane == base, 1.0, 0.0)).astype(BF16)


def _fox_prep(main, aux, q_gain, k_gain, b_f, bsz, seq, tm):
    nt = seq // tm
    cq = COL_FOX // MIX
    out = jax.ShapeDtypeStruct((bsz, N_HEADS, seq, LANES), BF16)
    ospec = pl.BlockSpec((1, N_HEADS, tm, LANES), lambda b, i: (b, 0, i, 0))
    row = lambda c: (lambda b, i: (b * nt + i, c))
    const = lambda b, i: (0, 0)
    return pl.pallas_call(
        _fox_prep_kernel,
        grid=(bsz, nt),
        in_specs=[pl.BlockSpec((tm, MIX), row(cq)),
                  pl.BlockSpec((tm, MIX), row(cq + 1)),
                  pl.BlockSpec((tm, MIX), row(cq + 2)),
                  pl.BlockSpec((tm, LANES), row(MIX // LANES)),
                  pl.BlockSpec((1, MIX), const),
                  pl.BlockSpec((1, MIX), const),
                  pl.BlockSpec((1, LANES), const),
                  pl.BlockSpec((MIX // 2, MIX // 2), const),
                  pl.BlockSpec((tm, tm), const)],
        out_specs=[ospec, ospec, ospec],
        out_shape=[out, out, out],
        scratch_shapes=[pltpu.VMEM((1, LANES), F32)],
        compiler_params=_cparams(("parallel", "arbitrary")),
        name="fox_prep",
    )(main, main, main, aux, q_gain, k_gain, b_f, _block_ones(MIX // 2), _block_tril(tm, tm))


def _fox_attn_kernel(q_ref, k_ref, v_ref, o_ref, *, tq):
    qi = pl.program_id(2)
    lane = lax.broadcasted_iota(jnp.int32, (tq, LANES), 1)
    rows = lax.broadcasted_iota(jnp.int32, (tq, tq), 0)
    cols = lax.broadcasted_iota(jnp.int32, (tq, tq), 1)
    causal = rows >= cols
    qs = [q_ref[0, hh] for hh in range(2)]

    def step(j, carry, masked):
        new = []
        for hh in range(2):
            m, acc = carry[hh]
            start = pl.multiple_of(j * tq, tq)
            kb = k_ref[0, hh, pl.ds(start, tq), :]
            vb = v_ref[0, hh, pl.ds(start, tq), :]
            s = _dot_nt(qs[hh], kb)
            if masked:
                s = jnp.where(causal, s, MASK_VALUE)
            m_new = jnp.maximum(m, jnp.max(s, axis=-1, keepdims=True))
            p = jnp.exp(s - m_new)
            acc = jnp.exp(m - m_new) * acc + jnp.dot(p.astype(BF16), vb, preferred_element_type=F32)
            new.append((m_new, acc))
        return tuple(new)

    init = tuple((jnp.full((tq, 1), MASK_VALUE, F32), jnp.zeros((tq, LANES), F32)) for _ in range(2))
    carry = lax.fori_loop(0, qi, lambda j, c: step(j, c, False), init)
    (_, acc0), (_, acc1) = step(qi, carry, True)
    den0 = jnp.sum(jnp.where(lane == HEAD, acc0, 0.0), axis=1, keepdims=True)
    den1 = jnp.sum(jnp.where(lane == 0, acc1, 0.0), axis=1, keepdims=True)
    o_ref[...] = jnp.where(lane < HEAD, acc0 / den0, acc1 / den1).astype(o_ref.dtype)


def _fox_attn(qh, kh, vh, tq):
    bsz, _, seq, _ = qh.shape
    nq = seq // tq
    kv_spec = pl.BlockSpec((1, 2, seq, LANES), lambda b, p, i: (b, p, 0, 0))
    return pl.pallas_call(
        functools.partial(_fox_attn_kernel, tq=tq),
        grid=(bsz, N_PAIRS, nq),
        in_specs=[pl.BlockSpec((1, 2, tq, LANES), lambda b, p, i: (b, p, i, 0)), kv_spec, kv_spec],
        out_specs=pl.BlockSpec((tq, LANES), lambda b, p, i: (b * nq + i, p)),
        out_shape=jax.ShapeDtypeStruct((bsz * seq, MIX), BF16),
        compiler_params=_cparams(("parallel", "parallel", "arbitrary")),
        name="fox_attn",
    )(qh, kh, vh)


def _hgrn_kernel(q_ref, i_ref, og_ref, f_ref, lb_ref, gain_ref, e_ref, tri_ref, o_ref,
                 s_ref, qd_ref, kd_ref, vb_ref, ge_ref, os_ref):
    tm = q_ref.shape[0]
    c = HGRN_SUB
    n = tm // c

    @pl.when(pl.program_id(1) == 0)
    def _():
        s_ref[...] = jnp.zeros_like(s_ref)

    lb = lb_ref[...]
    gate = lb + (1.0 - lb) * _sigmoid(f_ref[...])
    ell = jnp.log(jnp.maximum(gate, LOG_FLOOR))
    kx = 1.0 - gate
    qraw = q_ref[...].astype(F32)
    q = qraw * _sigmoid(qraw)
    v = i_ref[...].astype(F32)
    g = _exact_dot(tri_ref[...], ell)

    def row_of_chunk(a, j):
        a3 = a.reshape(n, c, MIX)
        return jnp.broadcast_to(a3[:, j:j + 1, :], (n, c, MIX)).reshape(tm, MIX)

    g_end = row_of_chunk(g, c - 1)
    qd_ref[...] = (q * jnp.exp(g)).astype(BF16)
    kd_ref[...] = (kx * jnp.exp(g_end - g)).astype(BF16)
    vb_ref[...] = v.astype(BF16)
    ge_ref[...] = g_end

    e = e_ref[...]
    pos = lax.broadcasted_iota(jnp.int32, (tm, MIX), 0) % c
    acc = jnp.zeros((tm, MIX), F32)
    for j in range(c):
        diff = jnp.where(pos >= j, g - row_of_chunk(g, j), MASK_VALUE)
        w = _group_sum(q * row_of_chunk(kx, j) * jnp.exp(diff), e)
        acc = acc + w * row_of_chunk(v, j)

    lane_r = lax.broadcasted_iota(jnp.int32, (LANES, LANES), 0) // HEAD
    lane_c = lax.broadcasted_iota(jnp.int32, (LANES, LANES), 1) // HEAD
    same_head = lane_r == lane_c

    def body(ci, carry):
        start = pl.multiple_of(ci * c, c)
        rows = pl.ds(start, c)
        for p in range(N_PAIRS):
            sl = slice(p * LANES, (p + 1) * LANES)
            st = s_ref[p]
            os_ref[rows, sl] = _dot_nt(qd_ref[rows, sl], st.astype(BF16))
            upd = _dot_tn(vb_ref[rows, sl], kd_ref[rows, sl])
            dec = jnp.exp(ge_ref[pl.ds(start, 1), sl])
            s_ref[p] = st * dec + jnp.where(same_head, upd, 0.0)
        return carry

    lax.fori_loop(0, n, body, 0)

    o = acc + os_ref[...]
    ms = _group_sum(o * o, e) * (1.0 / HEAD)
    og = og_ref[...].astype(F32)
    o_ref[...] = (o * lax.rsqrt(ms + NORM_EPS) * gain_ref[...] * (og * _sigmoid(og))).astype(o_ref.dtype)


def _hgrn(main, aux, lower, o_gain, bsz, seq, tm):
    nt = seq // tm
    ch = COL_HGRN // MIX
    row = lambda cidx: (lambda b, i: (b * nt + i, cidx))
    const = lambda b, i: (0, 0)
    return pl.pallas_call(
        _hgrn_kernel,
        grid=(bsz, nt),
        in_specs=[pl.BlockSpec((tm, MIX), row(ch)),
                  pl.BlockSpec((tm, MIX), row(ch + 1)),
                  pl.BlockSpec((tm, MIX), row(ch + 2)),
                  pl.BlockSpec((tm, MIX), row(0)),
                  pl.BlockSpec((1, MIX), const),
                  pl.BlockSpec((1, MIX), const),
                  pl.BlockSpec((MIX // 2, MIX // 2), const),
                  pl.BlockSpec((tm, tm), const)],
        out_specs=pl.BlockSpec((tm, MIX), lambda b, i: (b * nt + i, 0)),
        out_shape=jax.ShapeDtypeStruct((bsz * seq, MIX), BF16),
        scratch_shapes=[pltpu.VMEM((N_PAIRS, LANES, LANES), F32),
                        pltpu.VMEM((tm, MIX), BF16),
                        pltpu.VMEM((tm, MIX), BF16),
                        pltpu.VMEM((tm, MIX), BF16),
                        pltpu.VMEM((tm, MIX), F32),
                        pltpu.VMEM((tm, MIX), F32)],
        compiler_params=_cparams(("parallel", "arbitrary")),
        name="hgrn2",
    )(main, main, main, aux, lower, o_gain, _block_ones(MIX // 2), _block_tril(tm, HGRN_SUB))


def _rwkv_prep_kernel(*refs, has_vres):
    if has_vres:
        (p_ref, ph_ref, mu_ref, w0_ref, a0_ref, wa2_ref, g2_ref, kkk_ref, ka_ref, e_ref,
         misc_ref, misch_ref, vmu_ref, v0_ref, v2_ref, vf_ref,
         r_o, k_o, v_o, kk_o, a_o, lw_o, g_o) = refs
    else:
        (p_ref, ph_ref, mu_ref, w0_ref, a0_ref, wa2_ref, g2_ref, kkk_ref, ka_ref, e_ref,
         r_o, k_o, v_o, kk_o, a_o, lw_o, g_o) = refs
    tm = p_ref.shape[0]
    first = pl.program_id(1) == 0

    def token_shift(cur, halo_ref):
        prev_last = halo_ref[halo_ref.shape[0] - 1:, :].astype(F32)
        prev_last = jnp.where(first, 0.0, prev_last)
        rolled = pltpu.roll(cur, 1, 0)
        row = lax.broadcasted_iota(jnp.int32, cur.shape, 0)
        return jnp.where(row == 0, prev_last, rolled)

    p = p_ref[...].astype(F32)
    pm = p + (token_shift(p, ph_ref) - p) * mu_ref[...]
    r = pm[:, 0:MIX]
    k_raw = pm[:, MIX:2 * MIX]
    v = pm[:, 2 * MIX:3 * MIX]
    lora = pm[:, 3 * MIX:3 * MIX + LANES]
    g_lo = pm[:, 3 * MIX + LANES:3 * MIX + 2 * LANES]
    lane = lax.broadcasted_iota(jnp.int32, (tm, LANES), 1)
    z = jnp.where(lane < HEAD, jnp.tanh(lora), lora)
    wa = jnp.dot(z.astype(BF16), wa2_ref[...], preferred_element_type=F32)
    w_raw = _log_sigmoid(w0_ref[...] + wa[:, :MIX]) - 0.5
    lw_o[...] = -jnp.exp(w_raw)
    a = _sigmoid(a0_ref[...] + wa[:, MIX:])
    g_o[...] = jnp.dot(_sigmoid(g_lo).astype(BF16), g2_ref[...], preferred_element_type=F32).astype(BF16)
    if has_vres:
        m = misc_ref[...]
        ms = m + (token_shift(m, misch_ref) - m) * vmu_ref[...]
        mix = _sigmoid(v0_ref[...] + jnp.dot(ms.astype(BF16), v2_ref[...], preferred_element_type=F32))
        v = v + (vf_ref[...].astype(F32) - v) * mix
    kk = k_raw * kkk_ref[...]
    ss = _group_sum(kk * kk, e_ref[...])
    kk_o[...] = (kk * lax.rsqrt(jnp.maximum(ss, 1e-24))).astype(BF16)
    r_o[...] = r.astype(BF16)
    k_o[...] = (k_raw * (1.0 + (a - 1.0) * ka_ref[...])).astype(BF16)
    v_o[...] = v.astype(BF16)
    a_o[...] = a.astype(BF16)


def _rwkv_prep(main, aux, prm, v_first, bsz, seq, tm):
    nt = seq // tm
    has_vres = v_first is not None
    pw = 4 * MIX
    cp = COL_RWKV // pw
    halo = 16
    row = lambda cidx: (lambda b, i: (b * nt + i, cidx))
    const = lambda b, i: (0, 0)

    def halo_map(rows, cidx):
        per = tm // rows
        return lambda b, i: (jnp.maximum((b * nt + i) * per - 1, 0), cidx)

    in_specs = [pl.BlockSpec((tm, pw), row(cp)),
                pl.BlockSpec((halo, pw), halo_map(halo, cp)),
                pl.BlockSpec((1, pw), const),
                pl.BlockSpec((1, MIX), const),
                pl.BlockSpec((1, MIX), const),
                pl.BlockSpec((LANES, 2 * MIX), const),
                pl.BlockSpec((LANES, MIX), const),
                pl.BlockSpec((1, MIX), const),
                pl.BlockSpec((1, MIX), const),
                pl.BlockSpec((MIX // 2, MIX // 2), const)]
    args = [main, main, prm["mu"], prm["w0"], prm["a0"], prm["wa2"], prm["g2"], prm["k_k"], prm["k_a"],
            _block_ones(MIX // 2)]
    if has_vres:
        in_specs += [pl.BlockSpec((tm, LANES), row(MIX // LANES)),
                     pl.BlockSpec((8, LANES), halo_map(8, MIX // LANES)),
                     pl.BlockSpec((1, LANES), const),
                     pl.BlockSpec((1, MIX), const),
                     pl.BlockSpec((LANES, MIX), const),
                     pl.BlockSpec((tm, MIX), row(0))]
        args += [aux, aux, prm["vmu"], prm["v0"], prm["v2"], v_first]
    ospec = pl.BlockSpec((tm, MIX), lambda b, i: (b * nt + i, 0))
    o16 = jax.ShapeDtypeStruct((bsz * seq, MIX), BF16)
    o32 = jax.ShapeDtypeStruct((bsz * seq, MIX), F32)
    return pl.pallas_call(
        functools.partial(_rwkv_prep_kernel, has_vres=has_vres),
        grid=(bsz, nt),
        in_specs=in_specs,
        out_specs=[ospec] * 7,
        out_shape=[o16, o16, o16, o16, o16, o32, o16],
        compiler_params=_cparams(("parallel", "arbitrary")),
        name="rwkv_prep",
    )(*args)


def _rwkv_scan_kernel(r_ref, k_ref, v_ref, kk_ref, a_ref, lw_ref, g_ref, rk_ref, lnw_ref, lnb_ref,
                      tri_ref, e_ref, o_ref, s_ref):
    tm = r_ref.shape[0]
    c = RWKV_CHUNK

    @pl.when(pl.program_id(1) == 0)
    def _():
        s_ref[...] = jnp.zeros_like(s_ref)

    tri = tri_ref[...]
    e = e_ref[...]
    lane = lax.broadcasted_iota(jnp.int32, (c, LANES), 1)
    t_idx = lax.broadcasted_iota(jnp.int32, (c, LANES), 0)
    s_idx = lane % HEAD
    left = lane < HEAD
    strict = t_idx > s_idx
    incl = t_idx >= s_idx
    eye = jnp.where(t_idx == s_idx, 1.0, 0.0)
    lane_r = lax.broadcasted_iota(jnp.int32, (LANES, LANES), 0) // HEAD
    lane_c = lax.broadcasted_iota(jnp.int32, (LANES, LANES), 1) // HEAD
    same_head = lane_r == lane_c

    def halves(x):
        return jnp.where(left, x, 0.0), jnp.where(left, 0.0, x)

    def bdiag(x):
        x0, x1 = halves(x)
        return jnp.concatenate([x0, x1], axis=0).astype(BF16)

    def pmm(a, b_bd):
        return jnp.dot(a.astype(BF16), b_bd, preferred_element_type=F32)

    def chunk(ci, carry):
        start = pl.multiple_of(ci * c, c)
        rows = pl.ds(start, c)
        for p in range(N_PAIRS):
            sl = slice(p * LANES, (p + 1) * LANES)
            r = r_ref[rows, sl].astype(F32)
            k = k_ref[rows, sl].astype(F32)
            v = v_ref[rows, sl].astype(F32)
            kk = kk_ref[rows, sl].astype(F32)
            a = a_ref[rows, sl].astype(F32)
            lw = lw_ref[rows, sl]
            st = s_ref[p]

            gam = _exact_dot(tri, lw)
            g_last = gam[c - 1:c, :]
            e_inv = jnp.exp(-gam)
            b = a * kk
            kq = kk * jnp.exp(gam - lw)
            rq = r * jnp.exp(gam)
            bk = b * e_inv
            kkey = k * e_inv
            e_dec = jnp.exp(g_last - gam)

            lhs = jnp.concatenate([kq, rq], axis=0).astype(BF16)
            bk0, bk1 = halves(bk)
            kk0, kk1 = halves(kkey)
            rhs = jnp.concatenate([bk0, bk1, kk0, kk1], axis=0).astype(BF16)
            aa = _dot_nt(lhs, rhs)
            a_ab = jnp.where(strict, aa[:c, :LANES], 0.0)
            a_ak = jnp.where(strict, aa[:c, LANES:], 0.0)
            a_rb = jnp.where(incl, aa[c:, :LANES], 0.0)
            a_rk = jnp.where(incl, aa[c:, LANES:], 0.0)

            npow = -a_ab
            tinv = eye + npow
            for _ in range(5):
                nb = bdiag(npow)
                npow = pmm(npow, nb)
                tinv = tinv + pmm(tinv, bdiag(npow))

            x = _dot_nt(lhs, st.astype(BF16))
            v_bd = bdiag(v)
            y = x[:c] + pmm(a_ak, v_bd)
            u = pmm(tinv, bdiag(y))
            o = x[c:] + pmm(a_rk, v_bd) - pmm(a_rb, bdiag(u))

            wv = jnp.concatenate([u, v], axis=0).astype(BF16)
            wk = jnp.concatenate([-(b * e_dec), k * e_dec], axis=0).astype(BF16)
            s_ref[p] = st * jnp.exp(g_last) + jnp.where(same_head, _dot_tn(wv, wk), 0.0)

            mean = _group_sum2(o, e) * (1.0 / HEAD)
            d = o - mean
            var = _group_sum2(d * d, e) * (1.0 / HEAD)
            bonus = _group_sum2(r * k * rk_ref[:, sl], e)
            out = d * lax.rsqrt(var + RWKV_LN_EPS) * lnw_ref[:, sl] + lnb_ref[:, sl] + bonus * v
            o_ref[rows, sl] = (out * g_ref[rows, sl].astype(F32)).astype(o_ref.dtype)
        return carry

    lax.fori_loop(0, tm // c, chunk, 0)


def _rwkv_scan(r, k, v, kk, a, lw, g, r_k, ln_w, ln_b, bsz, seq, tm):
    nt = seq // tm
    spec = pl.BlockSpec((tm, MIX), lambda b, i: (b * nt + i, 0))
    const = lambda b, i: (0, 0)
    vec = pl.BlockSpec((1, MIX), const)
    return pl.pallas_call(
        _rwkv_scan_kernel,
        grid=(bsz, nt),
        in_specs=[spec] * 7 + [vec, vec, vec,
                               pl.BlockSpec((RWKV_CHUNK, RWKV_CHUNK), const),
                               pl.BlockSpec((LANES, LANES), const)],
        out_specs=spec,
        out_shape=jax.ShapeDtypeStruct((bsz * seq, MIX), BF16),
        scratch_shapes=[pltpu.VMEM((N_PAIRS, LANES, LANES), F32)],
        compiler_params=_cparams(("parallel", "arbitrary")),
        name="rwkv_scan",
    )(r, k, v, kk, a, lw, g, r_k, ln_w, ln_b, _block_tril(RWKV_CHUNK, RWKV_CHUNK), _block_ones(LANES))


def _merge_kernel(yf_ref, yh_ref, yr_ref, gf_ref, gh_ref, gr_ref, x_ref, gt_ref, wb_ref, wo_ref, o_ref):
    def branch(y_ref, g_ref, i):
        return _sigmoid(g_ref[...].astype(F32)) * jnp.dot(y_ref[...], wb_ref[i], preferred_element_type=F32)

    merged = branch(yf_ref, gf_ref, 0) + branch(yh_ref, gh_ref, 1) + branch(yr_ref, gr_ref, 2)
    o_ref[...] = x_ref[...] + gt_ref[0] * jnp.dot(merged.astype(BF16), wo_ref[...], preferred_element_type=F32)


def _merge(y_fox, y_hgrn, y_rwkv, main, x, gt, w_branch, w_out, seq, tm):
    t, d = x.shape
    tps = seq // tm
    yspec = pl.BlockSpec((tm, MIX), lambda i: (i, 0))
    gspec = lambda cidx: pl.BlockSpec((tm, d), lambda i: (i, cidx))
    return pl.pallas_call(
        _merge_kernel,
        grid=(t // tm,),
        in_specs=[yspec, yspec, yspec, gspec(0), gspec(1), gspec(2),
                  pl.BlockSpec((tm, d), lambda i: (i, 0)),
                  pl.BlockSpec((1, 1, d), lambda i: (i // tps, 0, 0)),
                  pl.BlockSpec((3, MIX, d), lambda i: (0, 0, 0)),
                  pl.BlockSpec((d, d), lambda i: (0, 0))],
        out_specs=pl.BlockSpec((tm, d), lambda i: (i, 0)),
        out_shape=jax.ShapeDtypeStruct((t, d), F32),
        compiler_params=_cparams(("parallel",)),
        name="merge_out",
    )(y_fox, y_hgrn, y_rwkv, main, main, main, x, gt, w_branch, w_out)


FFN_HALO = 16


def _ffn_body(x_ref, xh_ref, sc_ref, sh_ref, gt_ref, wv_ref, wg_ref, cwv_ref, cwg_ref, cbv_ref, cbg_ref,
              wd_ref, o_ref, h_ref, acc_ref, *, tps):
    tm = x_ref.shape[0]
    j = pl.program_id(1)

    @pl.when(j == 0)
    def _():
        def normed(x):
            ms = jnp.mean(x * x, axis=-1, keepdims=True)
            return x * lax.rsqrt(ms + NORM_EPS) * sc_ref[0] + sh_ref[0]

        seq_start = pl.program_id(0) % tps == 0
        h_ref[0:FFN_HALO, :] = jnp.where(seq_start, 0.0, normed(xh_ref[...])).astype(BF16)
        h_ref[FFN_HALO:, :] = normed(x_ref[...]).astype(BF16)
        acc_ref[...] = jnp.zeros_like(acc_ref)

    h = h_ref[...]

    def conv(w_ref, cw_ref, cb_ref):
        u = jnp.dot(h, w_ref[...], preferred_element_type=F32)
        u1 = pltpu.roll(u, 1, 0)
        u2 = pltpu.roll(u, 2, 0)
        cw = cw_ref[...]
        y = cb_ref[...] + cw[0:1] * u2 + cw[1:2] * u1 + cw[2:3] * u
        return y[FFN_HALO:]

    val = conv(wv_ref, cwv_ref, cbv_ref)
    gate = conv(wg_ref, cwg_ref, cbg_ref)
    act = gate * _sigmoid(gate) * val
    acc_ref[...] += jnp.dot(act.astype(BF16), wd_ref[...], preferred_element_type=F32)

    @pl.when(j == pl.num_programs(1) - 1)
    def _():
        o_ref[...] = x_ref[...] + gt_ref[0] * acc_ref[...]


def _ffn(x, scale, shift, gt, w_up, conv_w, conv_b, w_down, seq, tm, tf):
    t, d = x.shape
    tps = seq // tm
    nj = D_FF // tf
    per = tm // FFN_HALO
    bmap = lambda i, j: (i // tps, 0, 0)
    return pl.pallas_call(
        functools.partial(_ffn_body, tps=tps),
        grid=(t // tm, nj),
        in_specs=[pl.BlockSpec((tm, d), lambda i, j: (i, 0)),
                  pl.BlockSpec((FFN_HALO, d), lambda i, j: (jnp.maximum(i * per - 1, 0), 0)),
                  pl.BlockSpec((1, 1, d), bmap),
                  pl.BlockSpec((1, 1, d), bmap),
                  pl.BlockSpec((1, 1, d), bmap),
                  pl.BlockSpec((d, tf), lambda i, j: (0, j)),
                  pl.BlockSpec((d, tf), lambda i, j: (0, nj + j)),
                  pl.BlockSpec((3, tf), lambda i, j: (0, j)),
                  pl.BlockSpec((3, tf), lambda i, j: (0, nj + j)),
                  pl.BlockSpec((1, tf), lambda i, j: (0, j)),
                  pl.BlockSpec((1, tf), lambda i, j: (0, nj + j)),
                  pl.BlockSpec((tf, d), lambda i, j: (j, 0))],
        out_specs=pl.BlockSpec((tm, d), lambda i, j: (i, 0)),
        out_shape=jax.ShapeDtypeStruct((t, d), F32),
        scratch_shapes=[pltpu.VMEM((tm + FFN_HALO, d), BF16), pltpu.VMEM((tm, d), F32)],
        compiler_params=_cparams(("parallel", "arbitrary")),
        name="conv_mlp",
    )(x, x, scale, shift, gt, w_up, w_up, conv_w, conv_w, conv_b, conv_b, w_down)


def _pack_in_proj(w_in, vres_down):
    depth, d, _ = w_in.shape
    fox, hgrn, rwkv, gate = 0, 1544, 3592, 5384
    z = lambda n: jnp.zeros((depth, d, n), w_in.dtype)
    main = jnp.concatenate([
        w_in[..., gate:gate + 3072],
        w_in[..., fox:fox + 1536],
        w_in[..., hgrn:hgrn + 512], w_in[..., hgrn + 1024:hgrn + 2048],
        w_in[..., rwkv:rwkv + 1792],
        z(N_MAIN - 7936)], axis=-1).astype(BF16)
    vres = jnp.concatenate([jnp.zeros((1, d, VRES_RANK), w_in.dtype), vres_down], axis=0)
    aux = jnp.concatenate([
        w_in[..., hgrn + 512:hgrn + 1024],
        w_in[..., fox + 1536:fox + 1544], z(AUX_VRES_LANE - 8),
        vres, z(LANES - AUX_VRES_LANE - VRES_RANK)], axis=-1).astype(BF16)
    return main, aux


def _tile_heads(v):
    return jnp.tile(v, N_HEADS).reshape(1, MIX)


def _forward(x, c, w_ada, b_ada, norm1_g, norm2_g, w_in, fox_b_f, fox_q_gain, fox_k_gain,
             hgrn_lb, hgrn_o_gain, rwkv_mu, rwkv_w0, rwkv_w2, rwkv_a0, rwkv_a2, rwkv_g2,
             rwkv_k_k, rwkv_k_a, rwkv_r_k, rwkv_ln_w, rwkv_ln_b, rwkv_vres_down, rwkv_vres_mu,
             rwkv_v0, rwkv_v2, w_branch, w_out, w_up, conv_w, conv_b, w_down, *, tiles):
    bsz, seq, d = x.shape
    depth = w_in.shape[0]
    t = bsz * seq

    lb_prob = jax.nn.softmax(hgrn_lb.astype(F32), axis=0)
    hgrn_lower = jnp.cumsum(lb_prob, axis=0) - lb_prob[0]

    mod = _ada_mod(c, w_ada, b_ada)
    w_main, w_aux = _pack_in_proj(w_in, rwkv_vres_down)
    w_branch16, w_out16 = w_branch.astype(BF16), w_out.astype(BF16)
    w_up16, w_down16 = w_up.astype(BF16), w_down.astype(BF16)

    xf = x.reshape(t, d)
    v_first = None
    for l in range(depth):
        sh1, sc1, gt1, sh2, sc2, gt2 = [m.reshape(bsz, 1, d) for m in jnp.split(mod[l], 6, axis=-1)]
        scale1 = norm1_g[l] * (1.0 + sc1)
        main = _norm_matmul(xf, scale1, sh1, w_main[l], BF16, seq, tiles["proj_tm"], tiles["proj_tn"], "in_proj")
        aux = _norm_matmul(xf, scale1, sh1, w_aux[l], F32, seq, tiles["proj_tm"], N_AUX, "in_proj_aux")

        b_f = jnp.zeros((1, LANES), F32).at[0, AUX_FF_LANE:AUX_FF_LANE + N_HEADS].set(fox_b_f[l])
        qh, kh, vh = _fox_prep(main, aux, _tile_heads(fox_q_gain[l]) * HEAD ** -0.5, _tile_heads(fox_k_gain[l]),
                               b_f, bsz, seq, tiles["fox_prep_tm"])
        y_fox = _fox_attn(qh, kh, vh, tiles["fox_tq"])

        y_hgrn = _hgrn(main, aux, hgrn_lower[l].reshape(1, MIX), _tile_heads(hgrn_o_gain[l]), bsz, seq,
                       tiles["hgrn_tm"])

        zpad = jnp.zeros((LANES - 2 * HEAD,), F32)
        wa2 = jnp.zeros((LANES, 2 * MIX), F32)
        wa2 = wa2.at[:HEAD, :MIX].set(rwkv_w2[l]).at[HEAD:, MIX:].set(rwkv_a2[l])
        prm = {
            "mu": jnp.concatenate([rwkv_mu[l], jnp.zeros((4 * MIX - rwkv_mu.shape[1],), F32)]).reshape(1, 4 * MIX),
            "w0": rwkv_w0[l].reshape(1, MIX), "a0": rwkv_a0[l].reshape(1, MIX),
            "wa2": wa2.astype(BF16), "g2": rwkv_g2[l].astype(BF16),
            "k_k": rwkv_k_k[l].reshape(1, MIX), "k_a": rwkv_k_a[l].reshape(1, MIX),
        }
        del zpad
        if l > 0:
            vmu = jnp.zeros((1, LANES), F32).at[0, AUX_VRES_LANE:AUX_VRES_LANE + VRES_RANK].set(rwkv_vres_mu[l - 1])
            v2 = jnp.zeros((LANES, MIX), F32).at[AUX_VRES_LANE:AUX_VRES_LANE + VRES_RANK].set(rwkv_v2[l - 1])
            prm.update(vmu=vmu, v0=rwkv_v0[l - 1].reshape(1, MIX), v2=v2.astype(BF16))
        r, k, v, kk, a, lw, g = _rwkv_prep(main, aux, prm, v_first, bsz, seq, tiles["rwkv_prep_tm"])
        if l == 0:
            v_first = v
        y_rwkv = _rwkv_scan(r, k, v, kk, a, lw, g, rwkv_r_k[l].reshape(1, MIX), rwkv_ln_w[l].reshape(1, MIX),
                            rwkv_ln_b[l].reshape(1, MIX), bsz, seq, tiles["rwkv_tm"])

        xf = _merge(y_fox, y_hgrn, y_rwkv, main, xf, gt1, w_branch16[l], w_out16[l], seq, tiles["merge_tm"])

        scale2 = norm2_g[l] * (1.0 + sc2)
        xf = _ffn(xf, scale2, sh2, gt2, w_up16[l], conv_w[l], conv_b[l].reshape(1, -1), w_down16[l], seq,
                  tiles["ffn_tm"], tiles["ffn_tf"])
    return xf.reshape(bsz, seq, d)


def _tiles_for(seq):
    cap = lambda n: min(n, seq)
    return dict(proj_tm=cap(1024), proj_tn=512, fox_prep_tm=cap(512), fox_tq=cap(256), hgrn_tm=cap(256),
                rwkv_prep_tm=cap(256), rwkv_tm=cap(256), merge_tm=cap(512), ffn_tm=cap(512), ffn_tf=256)


def kernel(x, c, w_ada, b_ada, norm1_g, norm2_g, w_in, fox_b_f, fox_q_gain, fox_k_gain, hgrn_lb, hgrn_o_gain, rwkv_mu, rwkv_w0, rwkv_w2, rwkv_a0, rwkv_a2, rwkv_g2, rwkv_k_k, rwkv_k_a, rwkv_r_k, rwkv_ln_w, rwkv_ln_b, rwkv_vres_down, rwkv_vres_mu, rwkv_v0, rwkv_v2, w_branch, w_out, w_up, conv_w, conv_b, w_down):
    return _forward(x, c, w_ada, b_ada, norm1_g, norm2_g, w_in, fox_b_f, fox_q_gain, fox_k_gain,
                    hgrn_lb, hgrn_o_gain, rwkv_mu, rwkv_w0, rwkv_w2, rwkv_a0, rwkv_a2, rwkv_g2,
                    rwkv_k_k, rwkv_k_a, rwkv_r_k, rwkv_ln_w, rwkv_ln_b, rwkv_vres_down, rwkv_vres_mu,
                    rwkv_v0, rwkv_v2, w_branch, w_out, w_up, conv_w, conv_b, w_down,
                    tiles=_tiles_for(x.shape[1]))
```

```python
import functools

import jax
import jax.numpy as jnp
import numpy as np
from jax import lax
from jax.experimental import pallas as pl
from jax.experimental.pallas import tpu as pltpu

F32 = jnp.float32
BF16 = jnp.bfloat16

D_MODEL = 1024
MIX = 512
HEAD = 64
N_HEADS = MIX // HEAD
N_PAIRS = N_HEADS // 2
LANES = 128
D_FF = 2816
NORM_EPS = 1e-6
RWKV_LN_EPS = 64e-5
MASK_VALUE = -1e30
LOG_FLOOR = 1e-30
VRES_RANK = 32

HGRN_SUB = 16
RWKV_CHUNK = 64

COL_GATE = 0
COL_FOX = 3072
COL_HGRN = 4608
COL_RWKV = 6144
N_MAIN = 8192
N_AUX = 640
AUX_FF_LANE = 0
AUX_VRES_LANE = 32

VMEM_LIMIT = 56 * 1024 * 1024


def _cparams(sem):
    return pltpu.CompilerParams(dimension_semantics=sem, vmem_limit_bytes=VMEM_LIMIT)


def _split3(x):
    hi = x.astype(BF16)
    r1 = x - hi.astype(F32)
    mid = r1.astype(BF16)
    lo = (r1 - mid.astype(F32)).astype(BF16)
    return hi, mid, lo


def _exact_dot(a01, x):
    hi, mid, lo = _split3(x)
    d = lambda b: jnp.dot(a01, b, preferred_element_type=F32)
    return d(hi) + d(mid) + d(lo)


def _group_sum(x, e):
    w = e.shape[0]
    xb = x.astype(BF16)
    if x.shape[1] == w:
        return jnp.dot(xb, e, preferred_element_type=F32)
    parts = [jnp.dot(xb[:, i:i + w], e, preferred_element_type=F32) for i in range(0, x.shape[1], w)]
    return jnp.concatenate(parts, axis=1)


def _group_sum2(x, e):
    hi = x.astype(BF16)
    lo = x - hi.astype(F32)
    return _group_sum(hi, e) + _group_sum(lo, e)


def _sigmoid(x):
    return 1.0 / (1.0 + jnp.exp(-x))


def _log_sigmoid(x):
    return jnp.minimum(x, 0.0) - jnp.log1p(jnp.exp(-jnp.abs(x)))


def _dot_nt(a, b):
    return lax.dot_general(a, b, (((1,), (1,)), ((), ())), preferred_element_type=F32)


def _dot_tn(a, b):
    return lax.dot_general(a, b, (((0,), (0,)), ((), ())), preferred_element_type=F32)


def _run_lockstep(gens):
    live = list(gens)
    while live:
        nxt = []
        for g in live:
            try:
                next(g)
                nxt.append(g)
            except StopIteration:
                pass
        live = nxt


def _block_ones(width):
    g = np.arange(width) // HEAD
    return jnp.asarray(g[:, None] == g[None, :], dtype=BF16)


def _block_tril(n, blk):
    i = np.arange(n)
    m = (i[:, None] >= i[None, :]) & ((i[:, None] // blk) == (i[None, :] // blk))
    return jnp.asarray(m, dtype=BF16)


def _ada_kernel(c_ref, w_ref, b_ref, o_ref):
    c = c_ref[...]
    cond = c * _sigmoid(c)
    o_ref[0] = jnp.dot(cond.astype(BF16), w_ref[0].astype(BF16), preferred_element_type=F32) + b_ref[0]


def _ada_mod(c, w_ada, b_ada, tn=1536):
    depth, d, n = w_ada.shape
    bsz = c.shape[0]
    return pl.pallas_call(
        _ada_kernel,
        grid=(depth, n // tn),
        in_specs=[pl.BlockSpec((bsz, d), lambda l, j: (0, 0)),
                  pl.BlockSpec((1, d, tn), lambda l, j: (l, 0, j)),
                  pl.BlockSpec((1, 1, tn), lambda l, j: (l, 0, j))],
        out_specs=pl.BlockSpec((1, bsz, tn), lambda l, j: (l, 0, j)),
        out_shape=jax.ShapeDtypeStruct((depth, bsz, n), F32),
        compiler_params=_cparams(("parallel", "parallel")),
        name="ada_mod",
    )(c, w_ada, b_ada.reshape(depth, 1, n))


def _norm_mm_kernel(x_ref, sc_ref, sh_ref, w_ref, o_ref, h_ref):
    @pl.when(pl.program_id(1) == 0)
    def _():
        x = x_ref[...]
        ms = jnp.mean(x * x, axis=-1, keepdims=True)
        h_ref[...] = (x * lax.rsqrt(ms + NORM_EPS) * sc_ref[0] + sh_ref[0]).astype(BF16)

    o_ref[...] = jnp.dot(h_ref[...], w_ref[...], preferred_element_type=F32).astype(o_ref.dtype)


def _norm_matmul(x, scale, shift, w, out_dtype, seq, tm, tn, name):
    t, d = x.shape
    n = w.shape[1]
    tps = seq // tm
    return pl.pallas_call(
        _norm_mm_kernel,
        grid=(t // tm, n // tn),
        in_specs=[pl.BlockSpec((tm, d), lambda i, j: (i, 0)),
                  pl.BlockSpec((1, 1, d), lambda i, j: (i // tps, 0, 0)),
                  pl.BlockSpec((1, 1, d), lambda i, j: (i // tps, 0, 0)),
                  pl.BlockSpec((d, tn), lambda i, j: (0, j))],
        out_specs=pl.BlockSpec((tm, tn), lambda i, j: (i, j)),
        out_shape=jax.ShapeDtypeStruct((t, n), out_dtype),
        scratch_shapes=[pltpu.VMEM((tm, d), BF16)],
        compiler_params=_cparams(("parallel", "arbitrary")),
        name=name,
    )(x, scale, shift, w)


def _fox_prep_kernel(q_ref, k_ref, v_ref, misc_ref, qg_ref, kg_ref, bf_ref, e_ref, tri_ref,
                     qo_ref, ko_ref, vo_ref, carry_ref):
    tm = q_ref.shape[0]

    @pl.when(pl.program_id(1) == 0)
    def _():
        carry_ref[...] = jnp.zeros_like(carry_ref)

    e = e_ref[...]

    def qk_norm(ref, gain):
        xx = ref[...].astype(F32)
        ss = _group_sum(xx * xx, e)
        return xx * lax.rsqrt(ss * (1.0 / HEAD) + NORM_EPS) * gain

    qn = qk_norm(q_ref, qg_ref[...])
    kn = qk_norm(k_ref, kg_ref[...])
    vv = v_ref[...].astype(F32)

    lf = _log_sigmoid(misc_ref[...] + bf_ref[...])
    cum = _exact_dot(tri_ref[...], lf) + carry_ref[...]
    carry_ref[...] = cum[tm - 1:tm, :]
    hi, mid, lo = [s.astype(F32) for s in _split3(cum)]

    lane = lax.broadcasted_iota(jnp.int32, (tm, LANES), 1)

    def column(a, h):
        return jnp.sum(jnp.where(lane == h, a, 0.0), axis=1, keepdims=True)

    for p in range(N_PAIRS):
        sl = slice(p * LANES, (p + 1) * LANES)
        qp, kp, vp = qn[:, sl], kn[:, sl], vv[:, sl]
        for par in range(2):
            h = 2 * p + par
            fh, fm, fl = column(hi, h), column(mid, h), column(lo, h)
            base = HEAD if par == 0 else 0
            keep = (lane < HEAD) if par == 0 else (lane >= HEAD)
            is_f = (lane >= base) & (lane < base + 3)
            is_g = (lane >= base + 3) & (lane < base + 6)
            fsel = jnp.where(lane == base, fh, jnp.where(lane == base + 1, fm, fl))
            gsel = jnp.where(lane == base + 3, fh, jnp.where(lane == base + 4, fm, fl))
            ext_q = jnp.where(is_f, fsel, jnp.where(is_g, 1.0, 0.0))
            ext_k = jnp.where(is_f, 1.0, jnp.where(is_g, -gsel, 0.0))
            qo_ref[0, h] = jnp.where(keep, qp, ext_q).T.astype(BF16)
            ko_ref[0, h] = jnp.where(keep, kp, ext_k).astype(BF16)
            vo_ref[0, h] = jnp.where(keep, vp, jnp.where(lane == base, 1.0, 0.0)).T.astype(BF16)


def _fox_prep(main, aux, q_gain, k_gain, b_f, bsz, seq, tm):
    nt = seq // tm
    cq = COL_FOX // MIX
    out = jax.ShapeDtypeStruct((bsz, N_HEADS, seq, LANES), BF16)
    out_t = jax.ShapeDtypeStruct((bsz, N_HEADS, LANES, seq), BF16)
    ospec = pl.BlockSpec((1, N_HEADS, tm, LANES), lambda b, i: (b, 0, i, 0))
    ospec_t = pl.BlockSpec((1, N_HEADS, LANES, tm), lambda b, i: (b, 0, 0, i))
    row = lambda c: (lambda b, i: (b * nt + i, c))
    const = lambda b, i: (0, 0)
    return pl.pallas_call(
        _fox_prep_kernel,
        grid=(bsz, nt),
        in_specs=[pl.BlockSpec((tm, MIX), row(cq)),
                  pl.BlockSpec((tm, MIX), row(cq + 1)),
                  pl.BlockSpec((tm, MIX), row(cq + 2)),
                  pl.BlockSpec((tm, LANES), row(MIX // LANES)),
                  pl.BlockSpec((1, MIX), const),
                  pl.BlockSpec((1, MIX), const),
                  pl.BlockSpec((1, LANES), const),
                  pl.BlockSpec((MIX // 2, MIX // 2), const),
                  pl.BlockSpec((tm, tm), const)],
        out_specs=[ospec_t, ospec, ospec_t],
        out_shape=[out_t, out, out_t],
        scratch_shapes=[pltpu.VMEM((1, LANES), F32)],
        compiler_params=_cparams(("parallel", "arbitrary")),
        name="fox_prep",
    )(main, main, main, aux, q_gain, k_gain, b_f, _block_ones(MIX // 2), _block_tril(tm, tm))


def _fox_attn_kernel(qt_ref, k_ref, vt_ref, o_ref, *, tq, sub, tk):
    qi = pl.program_id(2)
    n_sub = tq // sub
    kpt = tq // tk
    chains = [(hh, r) for hh in range(2) for r in range(n_sub)]

    def step(j, carry, key_off):
        new = list(carry)
        start = pl.multiple_of(j * tk, tk)

        def chain(idx, hh, r):
            q_lo = r * sub
            nk = tk if key_off is None else min(q_lo + sub - key_off, tk)
            if nk <= 0:
                return
            m, acc = carry[idx]
            qt = qt_ref[0, hh, :, q_lo:q_lo + sub]
            kb = k_ref[0, hh, pl.ds(start, nk), :]
            vtb = vt_ref[0, hh, :, pl.ds(start, nk)]
            st = jnp.dot(kb, qt, preferred_element_type=F32)
            yield
            if key_off is not None and key_off + nk > q_lo:
                kpos = lax.broadcasted_iota(jnp.int32, (nk, sub), 0) + key_off
                qpos = lax.broadcasted_iota(jnp.int32, (nk, sub), 1) + q_lo
                st = jnp.where(kpos <= qpos, st, MASK_VALUE)
            m_new = jnp.maximum(m, jnp.max(st, axis=0, keepdims=True))
            p = jnp.exp(st - m_new).astype(BF16)
            yield
            pv = jnp.dot(vtb, p, preferred_element_type=F32)
            yield
            new[idx] = (m_new, jnp.exp(m - m_new) * acc + pv)

        _run_lockstep([chain(i, hh, r) for i, (hh, r) in enumerate(chains)])
        return tuple(new)

    init = tuple((jnp.full((1, sub), MASK_VALUE, F32), jnp.zeros((LANES, sub), F32)) for _ in chains)
    carry = lax.fori_loop(0, qi * kpt, lambda j, c: step(j, c, None), init)
    for jd in range(kpt):
        carry = step(qi * kpt + jd, carry, jd * tk)
    row = lax.broadcasted_iota(jnp.int32, (LANES, tq), 0)
    accs = [jnp.concatenate([carry[hh * n_sub + r][1] for r in range(n_sub)], axis=1) for hh in range(2)]
    ot = jnp.where(row < HEAD, accs[0] / accs[0][HEAD:HEAD + 1, :], accs[1] / accs[1][0:1, :])
    o_ref[...] = ot.T.astype(o_ref.dtype)


def _fox_attn(qt, kh, vt, tq, sub, tk):
    bsz, _, seq, _ = kh.shape
    nq = seq // tq
    return pl.pallas_call(
        functools.partial(_fox_attn_kernel, tq=tq, sub=sub, tk=tk),
        grid=(bsz, N_PAIRS, nq),
        in_specs=[pl.BlockSpec((1, 2, LANES, tq), lambda b, p, i: (b, p, 0, i)),
                  pl.BlockSpec((1, 2, seq, LANES), lambda b, p, i: (b, p, 0, 0)),
                  pl.BlockSpec((1, 2, LANES, seq), lambda b, p, i: (b, p, 0, 0))],
        out_specs=pl.BlockSpec((tq, LANES), lambda b, p, i: (b * nq + i, p)),
        out_shape=jax.ShapeDtypeStruct((bsz * seq, MIX), BF16),
        compiler_params=_cparams(("parallel", "parallel", "arbitrary")),
        name="fox_attn",
    )(qt, kh, vt)


def _hgrn_kernel(q_ref, i_ref, og_ref, f_ref, lb_ref, gain_ref, e_ref, tri_ref, o_ref,
                 s_ref, qd_ref, kd_ref, vb_ref, ge_ref, os_ref):
    tm = q_ref.shape[0]
    c = HGRN_SUB
    n = tm // c

    @pl.when(pl.program_id(1) == 0)
    def _():
        s_ref[...] = jnp.zeros_like(s_ref)

    lb = lb_ref[...]
    gate = lb + (1.0 - lb) * _sigmoid(f_ref[...])
    ell = jnp.log(jnp.maximum(gate, LOG_FLOOR))
    kx = 1.0 - gate
    qraw = q_ref[...].astype(F32)
    q = qraw * _sigmoid(qraw)
    v = i_ref[...].astype(F32)
    g = _exact_dot(tri_ref[...], ell)

    def row_of_chunk(a, j):
        a3 = a.reshape(n, c, MIX)
        return jnp.broadcast_to(a3[:, j:j + 1, :], (n, c, MIX)).reshape(tm, MIX)

    g_end = row_of_chunk(g, c - 1)
    qd_ref[...] = (q * jnp.exp(g)).astype(BF16)
    kd_ref[...] = (kx * jnp.exp(g_end - g)).astype(BF16)
    vb_ref[...] = v.astype(BF16)
    ge_ref[...] = g_end

    e = e_ref[...]
    pos = lax.broadcasted_iota(jnp.int32, (tm, MIX), 0) % c
    acc = jnp.zeros((tm, MIX), F32)
    for j in range(c):
        diff = jnp.where(pos >= j, g - row_of_chunk(g, j), MASK_VALUE)
        w = _group_sum(q * row_of_chunk(kx, j) * jnp.exp(diff), e)
        acc = acc + w * row_of_chunk(v, j)

    lane_r = lax.broadcasted_iota(jnp.int32, (LANES, LANES), 0) // HEAD
    lane_c = lax.broadcasted_iota(jnp.int32, (LANES, LANES), 1) // HEAD
    same_head = lane_r == lane_c

    unroll = 4

    def body(ci, carry):
        starts = [pl.multiple_of((ci * unroll + u) * c, c) for u in range(unroll)]
        states = [s_ref[p] for p in range(N_PAIRS)]
        outs = {}

        def pair(p):
            sl = slice(p * LANES, (p + 1) * LANES)
            upds = []
            for u in range(unroll):
                rows = pl.ds(starts[u], c)
                upds.append(_dot_tn(vb_ref[rows, sl], kd_ref[rows, sl]))
                yield
            st = states[p]
            for u in range(unroll):
                rows = pl.ds(starts[u], c)
                outs[(p, u)] = _dot_nt(qd_ref[rows, sl], st.astype(BF16))
                st = st * jnp.exp(ge_ref[pl.ds(starts[u], 1), sl]) + jnp.where(same_head, upds[u], 0.0)
                yield
            states[p] = st

        _run_lockstep([pair(p) for p in range(N_PAIRS)])
        for p in range(N_PAIRS):
            sl = slice(p * LANES, (p + 1) * LANES)
            s_ref[p] = states[p]
            for u in range(unroll):
                os_ref[pl.ds(starts[u], c), sl] = outs[(p, u)]
        return carry

    lax.fori_loop(0, n // unroll, body, 0)

    o = acc + os_ref[...]
    ms = _group_sum(o * o, e) * (1.0 / HEAD)
    og = og_ref[...].astype(F32)
    o_ref[...] = (o * lax.rsqrt(ms + NORM_EPS) * gain_ref[...] * (og * _sigmoid(og))).astype(o_ref.dtype)


def _hgrn(main, aux, lower, o_gain, bsz, seq, tm):
    nt = seq // tm
    ch = COL_HGRN // MIX
    row = lambda cidx: (lambda b, i: (b * nt + i, cidx))
    const = lambda b, i: (0, 0)
    return pl.pallas_call(
        _hgrn_kernel,
        grid=(bsz, nt),
        in_specs=[pl.BlockSpec((tm, MIX), row(ch)),
                  pl.BlockSpec((tm, MIX), row(ch + 1)),
                  pl.BlockSpec((tm, MIX), row(ch + 2)),
                  pl.BlockSpec((tm, MIX), row(0)),
                  pl.BlockSpec((1, MIX), const),
                  pl.BlockSpec((1, MIX), const),
                  pl.BlockSpec((MIX // 2, MIX // 2), const),
                  pl.BlockSpec((tm, tm), const)],
        out_specs=pl.BlockSpec((tm, MIX), lambda b, i: (b * nt + i, 0)),
        out_shape=jax.ShapeDtypeStruct((bsz * seq, MIX), BF16),
        scratch_shapes=[pltpu.VMEM((N_PAIRS, LANES, LANES), F32),
                        pltpu.VMEM((tm, MIX), BF16),
                        pltpu.VMEM((tm, MIX), BF16),
                        pltpu.VMEM((tm, MIX), BF16),
                        pltpu.VMEM((tm, MIX), F32),
                        pltpu.VMEM((tm, MIX), F32)],
        compiler_params=_cparams(("parallel", "arbitrary")),
        name="hgrn2",
    )(main, main, main, aux, lower, o_gain, _block_ones(MIX // 2), _block_tril(tm, HGRN_SUB))


def _rwkv_prep_kernel(*refs, has_vres):
    if has_vres:
        (p_ref, ph_ref, mu_ref, w0_ref, a0_ref, wa2_ref, g2_ref, kkk_ref, ka_ref, e_ref,
         misc_ref, misch_ref, vmu_ref, v0_ref, v2_ref, vf_ref,
         r_o, k_o, v_o, kk_o, a_o, lw_o, g_o) = refs
    else:
        (p_ref, ph_ref, mu_ref, w0_ref, a0_ref, wa2_ref, g2_ref, kkk_ref, ka_ref, e_ref,
         r_o, k_o, v_o, kk_o, a_o, lw_o, g_o) = refs
    tm = p_ref.shape[0]
    first = pl.program_id(1) == 0

    def token_shift(cur, halo_ref):
        prev_last = halo_ref[halo_ref.shape[0] - 1:, :].astype(F32)
        prev_last = jnp.where(first, 0.0, prev_last)
        rolled = pltpu.roll(cur, 1, 0)
        row = lax.broadcasted_iota(jnp.int32, cur.shape, 0)
        return jnp.where(row == 0, prev_last, rolled)

    p = p_ref[...].astype(F32)
    pm = p + (token_shift(p, ph_ref) - p) * mu_ref[...]
    r = pm[:, 0:MIX]
    k_raw = pm[:, MIX:2 * MIX]
    v = pm[:, 2 * MIX:3 * MIX]
    lora = pm[:, 3 * MIX:3 * MIX + LANES]
    g_lo = pm[:, 3 * MIX + LANES:3 * MIX + 2 * LANES]
    lane = lax.broadcasted_iota(jnp.int32, (tm, LANES), 1)
    z = jnp.where(lane < HEAD, jnp.tanh(lora), lora)
    wa = jnp.dot(z.astype(BF16), wa2_ref[...], preferred_element_type=F32)
    w_raw = _log_sigmoid(w0_ref[...] + wa[:, :MIX]) - 0.5
    lw_o[...] = -jnp.exp(w_raw)
    a = _sigmoid(a0_ref[...] + wa[:, MIX:])
    g_o[...] = jnp.dot(_sigmoid(g_lo).astype(BF16), g2_ref[...], preferred_element_type=F32).astype(BF16)
    if has_vres:
        m = misc_ref[...]
        ms = m + (token_shift(m, misch_ref) - m) * vmu_ref[...]
        mix = _sigmoid(v0_ref[...] + jnp.dot(ms.astype(BF16), v2_ref[...], preferred_element_type=F32))
        v = v + (vf_ref[...].astype(F32) - v) * mix
    kk = k_raw * kkk_ref[...]
    ss = _group_sum(kk * kk, e_ref[...])
    kk_o[...] = (kk * lax.rsqrt(jnp.maximum(ss, 1e-24))).astype(BF16)
    r_o[...] = r.astype(BF16)
    k_o[...] = (k_raw * (1.0 + (a - 1.0) * ka_ref[...])).astype(BF16)
    v_o[...] = v.astype(BF16)
    a_o[...] = a.astype(BF16)


def _rwkv_prep(main, aux, prm, v_first, bsz, seq, tm):
    nt = seq // tm
    has_vres = v_first is not None
    pw = 4 * MIX
    cp = COL_RWKV // pw
    halo = 16
    row = lambda cidx: (lambda b, i: (b * nt + i, cidx))
    const = lambda b, i: (0, 0)

    def halo_map(rows, cidx):
        per = tm // rows
        return lambda b, i: (jnp.maximum((b * nt + i) * per - 1, 0), cidx)

    in_specs = [pl.BlockSpec((tm, pw), row(cp)),
                pl.BlockSpec((halo, pw), halo_map(halo, cp)),
                pl.BlockSpec((1, pw), const),
                pl.BlockSpec((1, MIX), const),
                pl.BlockSpec((1, MIX), const),
                pl.BlockSpec((LANES, 2 * MIX), const),
                pl.BlockSpec((LANES, MIX), const),
                pl.BlockSpec((1, MIX), const),
                pl.BlockSpec((1, MIX), const),
                pl.BlockSpec((MIX // 2, MIX // 2), const)]
    args = [main, main, prm["mu"], prm["w0"], prm["a0"], prm["wa2"], prm["g2"], prm["k_k"], prm["k_a"],
            _block_ones(MIX // 2)]
    if has_vres:
        in_specs += [pl.BlockSpec((tm, LANES), row(MIX // LANES)),
                     pl.BlockSpec((8, LANES), halo_map(8, MIX // LANES)),
                     pl.BlockSpec((1, LANES), const),
                     pl.BlockSpec((1, MIX), const),
                     pl.BlockSpec((LANES, MIX), const),
                     pl.BlockSpec((tm, MIX), row(0))]
        args += [aux, aux, prm["vmu"], prm["v0"], prm["v2"], v_first]
    ospec = pl.BlockSpec((tm, MIX), lambda b, i: (b * nt + i, 0))
    o16 = jax.ShapeDtypeStruct((bsz * seq, MIX), BF16)
    o32 = jax.ShapeDtypeStruct((bsz * seq, MIX), F32)
    return pl.pallas_call(
        functools.partial(_rwkv_prep_kernel, has_vres=has_vres),
        grid=(bsz, nt),
        in_specs=in_specs,
        out_specs=[ospec] * 7,
        out_shape=[o16, o16, o16, o16, o16, o32, o16],
        compiler_params=_cparams(("parallel", "arbitrary")),
        name="rwkv_prep",
    )(*args)


def _rwkv_scan_kernel(r_ref, k_ref, v_ref, kk_ref, a_ref, lw_ref, g_ref, rk_ref, lnw_ref, lnb_ref,
                      tri_ref, e_ref, o_ref,
                      s_ref, kq_s, rq_s, bk_s, kx_s, nb_s, kd_s, dec_s, tkq_s, rqe_s, w_s, ol_s, oo_s):
    tm = r_ref.shape[0]
    c = RWKV_CHUNK
    n = tm // c

    @pl.when(pl.program_id(1) == 0)
    def _():
        s_ref[...] = jnp.zeros_like(s_ref)

    lw = lw_ref[...]
    gam = _exact_dot(tri_ref[...], lw)
    g_last = jnp.broadcast_to(gam.reshape(n, c, MIX)[:, c - 1:c, :], (n, c, MIX)).reshape(tm, MIX)
    r = r_ref[...].astype(F32)
    k = k_ref[...].astype(F32)
    kk = kk_ref[...].astype(F32)
    b = a_ref[...].astype(F32) * kk
    e_inv = jnp.exp(-gam)
    e_dec = jnp.exp(g_last - gam)
    kq_s[...] = (kk * jnp.exp(gam - lw)).astype(BF16)
    rq_s[...] = (r * jnp.exp(gam)).astype(BF16)
    bk_s[...] = (b * e_inv).astype(BF16)
    kx_s[...] = (k * e_inv).astype(BF16)
    nb_s[...] = (-(b * e_dec)).astype(BF16)
    kd_s[...] = (k * e_dec).astype(BF16)
    dec_s[...] = jnp.exp(g_last)

    lane = lax.broadcasted_iota(jnp.int32, (c, LANES), 1)
    t_idx = lax.broadcasted_iota(jnp.int32, (c, LANES), 0)
    s_idx = lane % HEAD
    left = lane < HEAD
    strict = t_idx > s_idx
    incl = t_idx >= s_idx
    eye = jnp.where(t_idx == s_idx, 1.0, 0.0)
    lane_r = lax.broadcasted_iota(jnp.int32, (LANES, LANES), 0) // HEAD
    lane_c = lax.broadcasted_iota(jnp.int32, (LANES, LANES), 1) // HEAD
    same_head = lane_r == lane_c

    def halves(x):
        zero = jnp.zeros_like(x)
        return jnp.where(left, x, zero), jnp.where(left, zero, x)

    def bdiag(x):
        x0, x1 = halves(x.astype(BF16))
        return jnp.concatenate([x0, x1], axis=0)

    def mm(a, rhs):
        return jnp.dot(a.astype(BF16), rhs, preferred_element_type=F32)

    group = 2

    def local_chain(start, p):
        rows = pl.ds(start, c)
        sl = slice(p * LANES, (p + 1) * LANES)
        kq, rq = kq_s[rows, sl], rq_s[rows, sl]
        lhs = jnp.concatenate([kq, rq], axis=0)
        rhs = jnp.concatenate(halves(bk_s[rows, sl]) + halves(kx_s[rows, sl]), axis=0)
        aa = _dot_nt(lhs, rhs)
        yield
        a_ab = jnp.where(strict, aa[:c, :LANES], 0.0)
        a_ak = jnp.where(strict, aa[:c, LANES:], 0.0)
        a_rb = jnp.where(incl, aa[c:, :LANES], 0.0)
        a_rk = jnp.where(incl, aa[c:, LANES:], 0.0)
        yo = mm(jnp.concatenate([a_ak, a_rk], axis=0), bdiag(v_ref[rows, sl]))
        yield
        pw = -a_ab
        tinv = eye + pw
        pw = mm(pw, bdiag(pw))
        yield
        for _ in range(4):
            res = mm(pw, jnp.concatenate([bdiag(tinv), bdiag(pw)], axis=1))
            yield
            tinv = tinv + res[:, :LANES]
            pw = res[:, LANES:]
        tinv = tinv + mm(pw, bdiag(tinv))
        yield
        tw = mm(tinv, jnp.concatenate([bdiag(kq), bdiag(yo[:c])], axis=1))
        yield
        ar = mm(a_rb, jnp.concatenate([bdiag(tw[:, :LANES]), bdiag(tw[:, LANES:])], axis=1))
        yield
        tkq_s[rows, sl] = tw[:, :LANES].astype(BF16)
        w_s[rows, sl] = tw[:, LANES:]
        rqe_s[rows, sl] = (rq.astype(F32) - ar[:, :LANES]).astype(BF16)
        ol_s[rows, sl] = yo[c:] - ar[:, LANES:]

    def local_body(gi, carry):
        chains = []
        for u in range(group):
            start = pl.multiple_of((gi * group + u) * c, c)
            chains += [local_chain(start, p) for p in range(N_PAIRS)]
        _run_lockstep(chains)
        return carry

    lax.fori_loop(0, n // group, local_body, 0)

    def seq_body(ci, carry):
        start = pl.multiple_of(ci * c, c)
        rows = pl.ds(start, c)
        states = [s_ref[p] for p in range(N_PAIRS)]

        def pair(p):
            sl = slice(p * LANES, (p + 1) * LANES)
            st = states[p]
            x = _dot_nt(jnp.concatenate([tkq_s[rows, sl], rqe_s[rows, sl]], axis=0), st.astype(BF16))
            yield
            u = x[:c] + w_s[rows, sl]
            oo_s[rows, sl] = x[c:] + ol_s[rows, sl]
            wv = jnp.concatenate([u.astype(BF16), v_ref[rows, sl]], axis=0)
            wk = jnp.concatenate([nb_s[rows, sl], kd_s[rows, sl]], axis=0)
            upd = _dot_tn(wv, wk)
            yield
            states[p] = st * dec_s[pl.ds(start, 1), sl] + jnp.where(same_head, upd, 0.0)

        _run_lockstep([pair(p) for p in range(N_PAIRS)])
        for p in range(N_PAIRS):
            s_ref[p] = states[p]
        return carry

    lax.fori_loop(0, n, seq_body, 0)

    e = e_ref[...]
    o = oo_s[...]
    v = v_ref[...].astype(F32)
    mean = _group_sum2(o, e) * (1.0 / HEAD)
    d = o - mean
    var = _group_sum2(d * d, e) * (1.0 / HEAD)
    bonus = _group_sum2(r * k * rk_ref[...], e)
    out = d * lax.rsqrt(var + RWKV_LN_EPS) * lnw_ref[...] + lnb_ref[...] + bonus * v
    o_ref[...] = (out * g_ref[...].astype(F32)).astype(o_ref.dtype)


def _rwkv_scan(r, k, v, kk, a, lw, g, r_k, ln_w, ln_b, bsz, seq, tm):
    nt = seq // tm
    spec = pl.BlockSpec((tm, MIX), lambda b, i: (b * nt + i, 0))
    const = lambda b, i: (0, 0)
    vec = pl.BlockSpec((1, MIX), const)
    s16 = pltpu.VMEM((tm, MIX), BF16)
    s32 = pltpu.VMEM((tm, MIX), F32)
    return pl.pallas_call(
        _rwkv_scan_kernel,
        grid=(bsz, nt),
        in_specs=[spec] * 7 + [vec, vec, vec,
                               pl.BlockSpec((tm, tm), const),
                               pl.BlockSpec((MIX // 2, MIX // 2), const)],
        out_specs=spec,
        out_shape=jax.ShapeDtypeStruct((bsz * seq, MIX), BF16),
        scratch_shapes=[pltpu.VMEM((N_PAIRS, LANES, LANES), F32),
                        s16, s16, s16, s16, s16, s16, s32, s16, s16, s32, s32, s32],
        compiler_params=_cparams(("parallel", "arbitrary")),
        name="rwkv_scan",
    )(r, k, v, kk, a, lw, g, r_k, ln_w, ln_b, _block_tril(tm, RWKV_CHUNK), _block_ones(MIX // 2))


def _merge_kernel(yf_ref, yh_ref, yr_ref, gf_ref, gh_ref, gr_ref, x_ref, gt_ref, wb_ref, wo_ref, o_ref):
    def branch(y_ref, g_ref, i):
        return _sigmoid(g_ref[...].astype(F32)) * jnp.dot(y_ref[...], wb_ref[i], preferred_element_type=F32)

    merged = branch(yf_ref, gf_ref, 0) + branch(yh_ref, gh_ref, 1) + branch(yr_ref, gr_ref, 2)
    o_ref[...] = x_ref[...] + gt_ref[0] * jnp.dot(merged.astype(BF16), wo_ref[...], preferred_element_type=F32)


def _merge(y_fox, y_hgrn, y_rwkv, main, x, gt, w_branch, w_out, seq, tm):
    t, d = x.shape
    tps = seq // tm
    yspec = pl.BlockSpec((tm, MIX), lambda i: (i, 0))
    gspec = lambda cidx: pl.BlockSpec((tm, d), lambda i: (i, cidx))
    return pl.pallas_call(
        _merge_kernel,
        grid=(t // tm,),
        in_specs=[yspec, yspec, yspec, gspec(0), gspec(1), gspec(2),
                  pl.BlockSpec((tm, d), lambda i: (i, 0)),
                  pl.BlockSpec((1, 1, d), lambda i: (i // tps, 0, 0)),
                  pl.BlockSpec((3, MIX, d), lambda i: (0, 0, 0)),
                  pl.BlockSpec((d, d), lambda i: (0, 0))],
        out_specs=pl.BlockSpec((tm, d), lambda i: (i, 0)),
        out_shape=jax.ShapeDtypeStruct((t, d), F32),
        compiler_params=_cparams(("parallel",)),
        name="merge_out",
    )(y_fox, y_hgrn, y_rwkv, main, main, main, x, gt, w_branch, w_out)


FFN_HALO = 16


FFN_TF = 256


def _ffn_body(x_ref, xh_ref, sc_ref, sh_ref, gt_ref, wup_ref, cw_ref, cb_ref, wd_ref, o_ref, h_ref, act_ref,
              *, tps):
    def normed(x):
        ms = jnp.mean(x * x, axis=-1, keepdims=True)
        return x * lax.rsqrt(ms + NORM_EPS) * sc_ref[0] + sh_ref[0]

    seq_start = pl.program_id(0) % tps == 0
    h_ref[0:FFN_HALO, :] = jnp.where(seq_start, 0.0, normed(xh_ref[...])).astype(BF16)
    h_ref[FFN_HALO:, :] = normed(x_ref[...]).astype(BF16)

    def up_proj(col):
        return jnp.dot(h_ref[...], wup_ref[:, col:col + FFN_TF], preferred_element_type=F32)

    def conv(u, col):
        cw = cw_ref[:, col:col + FFN_TF]
        y = cb_ref[:, col:col + FFN_TF] + cw[0:1] * pltpu.roll(u, 2, 0) + cw[1:2] * pltpu.roll(u, 1, 0) + cw[2:3] * u
        return y[FFN_HALO:]

    cols = [base + j * FFN_TF for j in range(D_FF // FFN_TF) for base in (0, D_FF)]
    u_next = up_proj(cols[0])
    val = None
    for i, col in enumerate(cols):
        u = u_next
        if i + 1 < len(cols):
            u_next = up_proj(cols[i + 1])
        y = conv(u, col)
        if i % 2 == 0:
            val = y
        else:
            act_ref[:, col - D_FF:col - D_FF + FFN_TF] = (y * _sigmoid(y) * val).astype(BF16)

    down = jnp.dot(act_ref[...], wd_ref[...], preferred_element_type=F32)
    o_ref[...] = x_ref[...] + gt_ref[0] * down


def _ffn(x, scale, shift, gt, w_up, conv_w, conv_b, w_down, seq, tm):
    t, d = x.shape
    tps = seq // tm
    per = tm // FFN_HALO
    bmap = lambda i: (i // tps, 0, 0)
    const = lambda i: (0, 0)
    resident = dict(pipeline_mode=pl.Buffered(1))
    return pl.pallas_call(
        functools.partial(_ffn_body, tps=tps),
        grid=(t // tm,),
        in_specs=[pl.BlockSpec((tm, d), lambda i: (i, 0)),
                  pl.BlockSpec((FFN_HALO, d), lambda i: (jnp.maximum(i * per - 1, 0), 0)),
                  pl.BlockSpec((1, 1, d), bmap),
                  pl.BlockSpec((1, 1, d), bmap),
                  pl.BlockSpec((1, 1, d), bmap),
                  pl.BlockSpec((d, 2 * D_FF), const, **resident),
                  pl.BlockSpec((3, 2 * D_FF), const),
                  pl.BlockSpec((1, 2 * D_FF), const),
                  pl.BlockSpec((D_FF, d), const, **resident)],
        out_specs=pl.BlockSpec((tm, d), lambda i: (i, 0)),
        out_shape=jax.ShapeDtypeStruct((t, d), F32),
        scratch_shapes=[pltpu.VMEM((tm + FFN_HALO, d), BF16), pltpu.VMEM((tm, D_FF), BF16)],
        compiler_params=_cparams(("parallel",)),
        name="conv_mlp",
    )(x, x, scale, shift, gt, w_up, conv_w, conv_b, w_down)


def _pack_in_proj(w_in, vres_down):
    depth, d, _ = w_in.shape
    fox, hgrn, rwkv, gate = 0, 1544, 3592, 5384
    z = lambda n: jnp.zeros((depth, d, n), w_in.dtype)
    main = jnp.concatenate([
        w_in[..., gate:gate + 3072],
        w_in[..., fox:fox + 1536],
        w_in[..., hgrn:hgrn + 512], w_in[..., hgrn + 1024:hgrn + 2048],
        w_in[..., rwkv:rwkv + 1792],
        z(N_MAIN - 7936)], axis=-1).astype(BF16)
    vres = jnp.concatenate([jnp.zeros((1, d, VRES_RANK), w_in.dtype), vres_down], axis=0)
    aux = jnp.concatenate([
        w_in[..., hgrn + 512:hgrn + 1024],
        w_in[..., fox + 1536:fox + 1544], z(AUX_VRES_LANE - 8),
        vres, z(LANES - AUX_VRES_LANE - VRES_RANK)], axis=-1).astype(BF16)
    return main, aux


def _tile_heads(v):
    return jnp.tile(v, N_HEADS).reshape(1, MIX)


def _forward(x, c, w_ada, b_ada, norm1_g, norm2_g, w_in, fox_b_f, fox_q_gain, fox_k_gain,
             hgrn_lb, hgrn_o_gain, rwkv_mu, rwkv_w0, rwkv_w2, rwkv_a0, rwkv_a2, rwkv_g2,
             rwkv_k_k, rwkv_k_a, rwkv_r_k, rwkv_ln_w, rwkv_ln_b, rwkv_vres_down, rwkv_vres_mu,
             rwkv_v0, rwkv_v2, w_branch, w_out, w_up, conv_w, conv_b, w_down, *, tiles):
    bsz, seq, d = x.shape
    depth = w_in.shape[0]
    t = bsz * seq

    lb_prob = jax.nn.softmax(hgrn_lb.astype(F32), axis=0)
    hgrn_lower = jnp.cumsum(lb_prob, axis=0) - lb_prob[0]

    mod = _ada_mod(c, w_ada, b_ada)
    w_main, w_aux = _pack_in_proj(w_in, rwkv_vres_down)
    w_branch16, w_out16 = w_branch.astype(BF16), w_out.astype(BF16)
    w_up16, w_down16 = w_up.astype(BF16), w_down.astype(BF16)

    xf = x.reshape(t, d)
    v_first = None
    for l in range(depth):
        sh1, sc1, gt1, sh2, sc2, gt2 = [m.reshape(bsz, 1, d) for m in jnp.split(mod[l], 6, axis=-1)]
        scale1 = norm1_g[l] * (1.0 + sc1)
        main = _norm_matmul(xf, scale1, sh1, w_main[l], BF16, seq, tiles["proj_tm"], tiles["proj_tn"], "in_proj")
        aux = _norm_matmul(xf, scale1, sh1, w_aux[l], F32, seq, tiles["proj_tm"], N_AUX, "in_proj_aux")

        b_f = jnp.zeros((1, LANES), F32).at[0, AUX_FF_LANE:AUX_FF_LANE + N_HEADS].set(fox_b_f[l])
        qh, kh, vh = _fox_prep(main, aux, _tile_heads(fox_q_gain[l]) * HEAD ** -0.5, _tile_heads(fox_k_gain[l]),
                               b_f, bsz, seq, tiles["fox_prep_tm"])
        y_fox = _fox_attn(qh, kh, vh, tiles["fox_tq"], tiles["fox_sub"], tiles["fox_tk"])

        y_hgrn = _hgrn(main, aux, hgrn_lower[l].reshape(1, MIX), _tile_heads(hgrn_o_gain[l]), bsz, seq,
                       tiles["hgrn_tm"])

        wa2 = jnp.zeros((LANES, 2 * MIX), F32)
        wa2 = wa2.at[:HEAD, :MIX].set(rwkv_w2[l]).at[HEAD:, MIX:].set(rwkv_a2[l])
        prm = {
            "mu": jnp.concatenate([rwkv_mu[l], jnp.zeros((4 * MIX - rwkv_mu.shape[1],), F32)]).reshape(1, 4 * MIX),
            "w0": rwkv_w0[l].reshape(1, MIX), "a0": rwkv_a0[l].reshape(1, MIX),
            "wa2": wa2.astype(BF16), "g2": rwkv_g2[l].astype(BF16),
            "k_k": rwkv_k_k[l].reshape(1, MIX), "k_a": rwkv_k_a[l].reshape(1, MIX),
        }
        if l > 0:
            vmu = jnp.zeros((1, LANES), F32).at[0, AUX_VRES_LANE:AUX_VRES_LANE + VRES_RANK].set(rwkv_vres_mu[l - 1])
            v2 = jnp.zeros((LANES, MIX), F32).at[AUX_VRES_LANE:AUX_VRES_LANE + VRES_RANK].set(rwkv_v2[l - 1])
            prm.update(vmu=vmu, v0=rwkv_v0[l - 1].reshape(1, MIX), v2=v2.astype(BF16))
        r, k, v, kk, a, lw, g = _rwkv_prep(main, aux, prm, v_first, bsz, seq, tiles["rwkv_prep_tm"])
        if l == 0:
            v_first = v
        y_rwkv = _rwkv_scan(r, k, v, kk, a, lw, g, rwkv_r_k[l].reshape(1, MIX), rwkv_ln_w[l].reshape(1, MIX),
                            rwkv_ln_b[l].reshape(1, MIX), bsz, seq, tiles["rwkv_tm"])

        xf = _merge(y_fox, y_hgrn, y_rwkv, main, xf, gt1, w_branch16[l], w_out16[l], seq, tiles["merge_tm"])

        scale2 = norm2_g[l] * (1.0 + sc2)
        xf = _ffn(xf, scale2, sh2, gt2, w_up16[l], conv_w[l], conv_b[l].reshape(1, -1), w_down16[l], seq,
                  tiles["ffn_tm"])
    return xf.reshape(bsz, seq, d)


def _tiles_for(seq):
    cap = lambda n: min(n, seq)
    return dict(proj_tm=cap(1024), proj_tn=1024, fox_prep_tm=cap(512), fox_tq=cap(1024), fox_sub=256,
                fox_tk=cap(512), hgrn_tm=cap(256), rwkv_prep_tm=cap(256), rwkv_tm=cap(512), merge_tm=cap(512),
                ffn_tm=cap(512))


def kernel(x, c, w_ada, b_ada, norm1_g, norm2_g, w_in, fox_b_f, fox_q_gain, fox_k_gain, hgrn_lb, hgrn_o_gain, rwkv_mu, rwkv_w0, rwkv_w2, rwkv_a0, rwkv_a2, rwkv_g2, rwkv_k_k, rwkv_k_a, rwkv_r_k, rwkv_ln_w, rwkv_ln_b, rwkv_vres_down, rwkv_vres_mu, rwkv_v0, rwkv_v2, w_branch, w_out, w_up, conv_w, conv_b, w_down):
    return _forward(x, c, w_ada, b_ada, norm1_g, norm2_g, w_in, fox_b_f, fox_q_gain, fox_k_gain,
                    hgrn_lb, hgrn_o_gain, rwkv_mu, rwkv_w0, rwkv_w2, rwkv_a0, rwkv_a2, rwkv_g2,
                    rwkv_k_k, rwkv_k_a, rwkv_r_k, rwkv_ln_w, rwkv_ln_b, rwkv_vres_down, rwkv_vres_mu,
                    rwkv_v0, rwkv_v2, w_branch, w_out, w_up, conv_w, conv_b, w_down,
                    tiles=_tiles_for(x.shape[1]))
```

```python
import functools

import jax
import jax.numpy as jnp
import numpy as np
from jax import lax
from jax.experimental import pallas as pl
from jax.experimental.pallas import tpu as pltpu

F32 = jnp.float32
BF16 = jnp.bfloat16

D_MODEL = 1024
MIX = 512
HEAD = 64
N_HEADS = MIX // HEAD
N_PAIRS = N_HEADS // 2
LANES = 128
D_FF = 2816
NORM_EPS = 1e-6
RWKV_LN_EPS = 64e-5
MASK_VALUE = -1e30
LOG_FLOOR = 1e-30
LOG2E = 1.4426950408889634
VRES_RANK = 32

HGRN_SUB = 16
RWKV_CHUNK = 64

COL_GATE = 0
COL_FOX = 3072
COL_HGRN = 4608
COL_RWKV = 6144
N_MAIN = 8192
N_AUX = 640
AUX_FF_LANE = 0
AUX_VRES_LANE = 32

VMEM_LIMIT = 56 * 1024 * 1024


def _cparams(sem):
    return pltpu.CompilerParams(dimension_semantics=sem, vmem_limit_bytes=VMEM_LIMIT)


def _split3(x):
    hi = x.astype(BF16)
    r1 = x - hi.astype(F32)
    mid = r1.astype(BF16)
    lo = (r1 - mid.astype(F32)).astype(BF16)
    return hi, mid, lo


def _exact_dot(a01, x):
    hi, mid, lo = _split3(x)
    d = lambda b: jnp.dot(a01, b, preferred_element_type=F32)
    return d(hi) + d(mid) + d(lo)


def _group_sum(x, e):
    w = e.shape[0]
    xb = x.astype(BF16)
    if x.shape[1] == w:
        return jnp.dot(xb, e, preferred_element_type=F32)
    parts = [jnp.dot(xb[:, i:i + w], e, preferred_element_type=F32) for i in range(0, x.shape[1], w)]
    return jnp.concatenate(parts, axis=1)


def _group_sum2(x, e):
    hi = x.astype(BF16)
    lo = x - hi.astype(F32)
    return _group_sum(hi, e) + _group_sum(lo, e)


def _sigmoid(x):
    return 1.0 / (1.0 + jnp.exp(-x))


def _log_sigmoid(x):
    return jnp.minimum(x, 0.0) - jnp.log1p(jnp.exp(-jnp.abs(x)))


def _dot_nt(a, b):
    return lax.dot_general(a, b, (((1,), (1,)), ((), ())), preferred_element_type=F32)


def _dot_tn(a, b):
    return lax.dot_general(a, b, (((0,), (0,)), ((), ())), preferred_element_type=F32)


def _run_lockstep(gens):
    live = list(gens)
    while live:
        nxt = []
        for g in live:
            try:
                next(g)
                nxt.append(g)
            except StopIteration:
                pass
        live = nxt


def _block_ones(width):
    g = np.arange(width) // HEAD
    return jnp.asarray(g[:, None] == g[None, :], dtype=BF16)


def _block_tril(n, blk):
    i = np.arange(n)
    m = (i[:, None] >= i[None, :]) & ((i[:, None] // blk) == (i[None, :] // blk))
    return jnp.asarray(m, dtype=BF16)


def _ada_kernel(c_ref, w_ref, b_ref, o_ref):
    c = c_ref[...]
    cond = c * _sigmoid(c)
    o_ref[0] = jnp.dot(cond.astype(BF16), w_ref[0].astype(BF16), preferred_element_type=F32) + b_ref[0]


def _ada_mod(c, w_ada, b_ada, tn=1536):
    depth, d, n = w_ada.shape
    bsz = c.shape[0]
    return pl.pallas_call(
        _ada_kernel,
        grid=(depth, n // tn),
        in_specs=[pl.BlockSpec((bsz, d), lambda l, j: (0, 0)),
                  pl.BlockSpec((1, d, tn), lambda l, j: (l, 0, j)),
                  pl.BlockSpec((1, 1, tn), lambda l, j: (l, 0, j))],
        out_specs=pl.BlockSpec((1, bsz, tn), lambda l, j: (l, 0, j)),
        out_shape=jax.ShapeDtypeStruct((depth, bsz, n), F32),
        compiler_params=_cparams(("parallel", "parallel")),
        name="ada_mod",
    )(c, w_ada, b_ada.reshape(depth, 1, n))


def _in_proj_kernel(x_ref, sc_ref, sh_ref, w_ref, wa_ref, o_ref, oa_ref, h_ref, *, n_main):
    j = pl.program_id(1)

    @pl.when(j == 0)
    def _():
        x = x_ref[...]
        ms = jnp.mean(x * x, axis=-1, keepdims=True)
        h_ref[...] = (x * lax.rsqrt(ms + NORM_EPS) * sc_ref[0] + sh_ref[0]).astype(BF16)

    @pl.when(j < n_main)
    def _():
        o_ref[...] = jnp.dot(h_ref[...], w_ref[...], preferred_element_type=F32).astype(o_ref.dtype)

    @pl.when(j == n_main)
    def _():
        oa_ref[...] = jnp.dot(h_ref[...], wa_ref[...], preferred_element_type=F32)


def _in_proj(x, scale, shift, w, w_aux, seq, tm, tn):
    t, d = x.shape
    n = w.shape[1]
    n_aux = w_aux.shape[1]
    n_main = n // tn
    tps = seq // tm
    col = lambda i, j: (0, jnp.minimum(j, n_main - 1))
    return pl.pallas_call(
        functools.partial(_in_proj_kernel, n_main=n_main),
        grid=(t // tm, n_main + 1),
        in_specs=[pl.BlockSpec((tm, d), lambda i, j: (i, 0)),
                  pl.BlockSpec((1, 1, d), lambda i, j: (i // tps, 0, 0)),
                  pl.BlockSpec((1, 1, d), lambda i, j: (i // tps, 0, 0)),
                  pl.BlockSpec((d, tn), col),
                  pl.BlockSpec((d, n_aux), lambda i, j: (0, 0))],
        out_specs=[pl.BlockSpec((tm, tn), lambda i, j: (i, jnp.minimum(j, n_main - 1))),
                   pl.BlockSpec((tm, n_aux), lambda i, j: (i, 0))],
        out_shape=[jax.ShapeDtypeStruct((t, n), BF16), jax.ShapeDtypeStruct((t, n_aux), F32)],
        scratch_shapes=[pltpu.VMEM((tm, d), BF16)],
        compiler_params=_cparams(("parallel", "arbitrary")),
        name="in_proj",
    )(x, scale, shift, w, w_aux)


def _fox_prep_kernel(q_ref, k_ref, v_ref, misc_ref, qg_ref, kg_ref, bf_ref, e_ref, tri_ref, sel_ref,
                     qo_ref, ko_ref, vo_ref, carry_ref):
    tm = q_ref.shape[0]

    @pl.when(pl.program_id(1) == 0)
    def _():
        carry_ref[...] = jnp.zeros_like(carry_ref)

    e = e_ref[...]

    def qk_norm(ref, gain):
        xx = ref[...].astype(F32)
        ss = _group_sum(xx * xx, e)
        return xx * lax.rsqrt(ss * (1.0 / HEAD) + NORM_EPS) * gain

    qn = qk_norm(q_ref, qg_ref[...])
    kn = qk_norm(k_ref, kg_ref[...])
    vv = v_ref[...].astype(F32)

    lf = _log_sigmoid(misc_ref[...] + bf_ref[...])
    cum = _exact_dot(tri_ref[...], lf) + carry_ref[...]
    carry_ref[...] = cum[tm - 1:tm, :]
    hi, mid, lo = [s.astype(F32) for s in _split3(cum)]
    lane = lax.broadcasted_iota(jnp.int32, (tm, LANES), 1)
    fcat = jnp.where(lane < 8, hi, jnp.where(lane < 16, pltpu.roll(mid, 8, 1),
                     jnp.where(lane < 24, pltpu.roll(lo, 16, 1), jnp.where(lane == 24, 1.0, 0.0))))
    ext = jnp.dot(fcat.astype(BF16), sel_ref[...], preferred_element_type=F32)

    for p in range(N_PAIRS):
        sl = slice(p * LANES, (p + 1) * LANES)
        qp, kp, vp = qn[:, sl], kn[:, sl], vv[:, sl]
        for par in range(2):
            h = 2 * p + par
            base = HEAD if par == 0 else 0
            keep = (lane < HEAD) if par == 0 else (lane >= HEAD)
            qo_ref[0, h] = jnp.where(keep, qp, ext[:, 2 * h * LANES:(2 * h + 1) * LANES]).T.astype(BF16)
            ko_ref[0, h] = jnp.where(keep, kp, ext[:, (2 * h + 1) * LANES:(2 * h + 2) * LANES]).astype(BF16)
            vo_ref[0, h] = jnp.where(keep, vp, jnp.where(lane == base, 1.0, 0.0)).T.astype(BF16)


def _fox_sel():
    sel = np.zeros((LANES, N_HEADS, 2, LANES), np.float32)
    for h in range(N_HEADS):
        base = HEAD if h % 2 == 0 else 0
        for i in range(3):
            sel[8 * i + h, h, 0, base + i] = 1.0
            sel[24, h, 0, base + 3 + i] = 1.0
            sel[24, h, 1, base + i] = 1.0
            sel[8 * i + h, h, 1, base + 3 + i] = -1.0
    return jnp.asarray(sel.reshape(LANES, N_HEADS * 2 * LANES), dtype=BF16)


def _fox_prep(main, aux, q_gain, k_gain, b_f, bsz, seq, tm):
    nt = seq // tm
    cq = COL_FOX // MIX
    out = jax.ShapeDtypeStruct((bsz, N_HEADS, seq, LANES), BF16)
    out_t = jax.ShapeDtypeStruct((bsz, N_HEADS, LANES, seq), BF16)
    ospec = pl.BlockSpec((1, N_HEADS, tm, LANES), lambda b, i: (b, 0, i, 0))
    ospec_t = pl.BlockSpec((1, N_HEADS, LANES, tm), lambda b, i: (b, 0, 0, i))
    row = lambda c: (lambda b, i: (b * nt + i, c))
    const = lambda b, i: (0, 0)
    return pl.pallas_call(
        _fox_prep_kernel,
        grid=(bsz, nt),
        in_specs=[pl.BlockSpec((tm, MIX), row(cq)),
                  pl.BlockSpec((tm, MIX), row(cq + 1)),
                  pl.BlockSpec((tm, MIX), row(cq + 2)),
                  pl.BlockSpec((tm, LANES), row(MIX // LANES)),
                  pl.BlockSpec((1, MIX), const),
                  pl.BlockSpec((1, MIX), const),
                  pl.BlockSpec((1, LANES), const),
                  pl.BlockSpec((MIX // 2, MIX // 2), const),
                  pl.BlockSpec((tm, tm), const),
                  pl.BlockSpec((LANES, N_HEADS * 2 * LANES), const)],
        out_specs=[ospec_t, ospec, ospec_t],
        out_shape=[out_t, out, out_t],
        scratch_shapes=[pltpu.VMEM((1, LANES), F32)],
        compiler_params=_cparams(("parallel", "arbitrary")),
        name="fox_prep",
    )(main, main, main, aux, q_gain, k_gain, b_f, _block_ones(MIX // 2), _block_tril(tm, tm), _fox_sel())


def _fox_attn_kernel(qt_ref, k_ref, vt_ref, o_ref, *, tq, sub, tk):
    qi = pl.program_id(2)
    n_sub = tq // sub
    kpt = tq // tk
    chains = [(hh, r) for hh in range(2) for r in range(n_sub)]

    def step(j, carry, key_off):
        new = list(carry)
        start = pl.multiple_of(j * tk, tk)

        def chain(idx, hh, r):
            q_lo = r * sub
            nk = tk if key_off is None else min(q_lo + sub - key_off, tk)
            if nk <= 0:
                return
            m, acc = carry[idx]
            qt = qt_ref[0, hh, :, q_lo:q_lo + sub]
            kb = k_ref[0, hh, pl.ds(start, nk), :]
            vtb = vt_ref[0, hh, :, pl.ds(start, nk)]
            st = jnp.dot(kb, qt, preferred_element_type=F32)
            yield
            if key_off is not None and key_off + nk > q_lo:
                kpos = lax.broadcasted_iota(jnp.int32, (nk, sub), 0) + key_off
                qpos = lax.broadcasted_iota(jnp.int32, (nk, sub), 1) + q_lo
                st = jnp.where(kpos <= qpos, st, MASK_VALUE)
            m_new = jnp.maximum(m, jnp.max(st, axis=0, keepdims=True))
            p = jnp.exp(st - m_new).astype(BF16)
            yield
            pv = jnp.dot(vtb, p, preferred_element_type=F32)
            yield
            new[idx] = (m_new, jnp.exp(m - m_new) * acc + pv)

        _run_lockstep([chain(i, hh, r) for i, (hh, r) in enumerate(chains)])
        return tuple(new)

    init = tuple((jnp.full((1, sub), MASK_VALUE, F32), jnp.zeros((LANES, sub), F32)) for _ in chains)
    carry = lax.fori_loop(0, qi * kpt, lambda j, c: step(j, c, None), init)
    for jd in range(kpt):
        carry = step(qi * kpt + jd, carry, jd * tk)
    row = lax.broadcasted_iota(jnp.int32, (LANES, tq), 0)
    accs = [jnp.concatenate([carry[hh * n_sub + r][1] for r in range(n_sub)], axis=1) for hh in range(2)]
    ot = jnp.where(row < HEAD, accs[0] / accs[0][HEAD:HEAD + 1, :], accs[1] / accs[1][0:1, :])
    o_ref[...] = ot.T.astype(o_ref.dtype)


def _fox_attn(qt, kh, vt, tq, sub, tk):
    bsz, _, seq, _ = kh.shape
    nq = seq // tq
    return pl.pallas_call(
        functools.partial(_fox_attn_kernel, tq=tq, sub=sub, tk=tk),
        grid=(bsz, N_PAIRS, nq),
        in_specs=[pl.BlockSpec((1, 2, LANES, tq), lambda b, p, i: (b, p, 0, i)),
                  pl.BlockSpec((1, 2, seq, LANES), lambda b, p, i: (b, p, 0, 0)),
                  pl.BlockSpec((1, 2, LANES, seq), lambda b, p, i: (b, p, 0, 0))],
        out_specs=pl.BlockSpec((tq, LANES), lambda b, p, i: (b * nq + i, p)),
        out_shape=jax.ShapeDtypeStruct((bsz * seq, MIX), BF16),
        compiler_params=_cparams(("parallel", "parallel", "arbitrary")),
        name="fox_attn",
    )(qt, kh, vt)


def _hgrn_kernel(q_ref, i_ref, og_ref, f_ref, lb_ref, gain_ref, e_ref, tri_ref, o_ref,
                 s_ref, qd_ref, kd_ref, vb_ref, ge_ref, os_ref):
    tm = q_ref.shape[0]
    c = HGRN_SUB
    n = tm // c

    @pl.when(pl.program_id(1) == 0)
    def _():
        s_ref[...] = jnp.zeros_like(s_ref)

    lb = lb_ref[...]
    gate = lb + (1.0 - lb) * _sigmoid(f_ref[...])
    ell = jnp.log(jnp.maximum(gate, LOG_FLOOR))
    kx = 1.0 - gate
    qraw = q_ref[...].astype(F32)
    q = qraw * _sigmoid(qraw)
    v = i_ref[...].astype(F32)
    g = _exact_dot(tri_ref[...], ell)

    def row_of_chunk(a, j):
        a3 = a.reshape(n, c, MIX)
        return jnp.broadcast_to(a3[:, j:j + 1, :], (n, c, MIX)).reshape(tm, MIX)

    g_end = row_of_chunk(g, c - 1)
    qd_ref[...] = (q * jnp.exp(g)).astype(BF16)
    kd_ref[...] = (kx * jnp.exp(g_end - g)).astype(BF16)
    vb_ref[...] = v.astype(BF16)
    ge_ref[...] = g_end

    e = e_ref[...]
    pos = lax.broadcasted_iota(jnp.int32, (tm, MIX), 0) % c
    lane_r = lax.broadcasted_iota(jnp.int32, (LANES, LANES), 0) // HEAD
    lane_c = lax.broadcasted_iota(jnp.int32, (LANES, LANES), 1) // HEAD
    same_head = lane_r == lane_c
    band = []

    def intra():
        g2 = g * LOG2E
        qb = q.astype(BF16)
        kxb = kx.astype(BF16)
        acc = jnp.zeros((tm, MIX), F32)
        for j in range(c):
            diff = jnp.where(pos >= j, g2 - row_of_chunk(g2, j), MASK_VALUE)
            w = _group_sum(qb * row_of_chunk(kxb, j) * jnp.exp2(diff).astype(BF16), e)
            yield
            acc = acc + w * row_of_chunk(v, j)
        band.append(acc)

    def inter():
        states = [s_ref[p] for p in range(N_PAIRS)]
        for ci in range(n):
            rows = slice(ci * c, (ci + 1) * c)
            for p in range(N_PAIRS):
                sl = slice(p * LANES, (p + 1) * LANES)
                upd = _dot_tn(vb_ref[rows, sl], kd_ref[rows, sl])
                os_ref[rows, sl] = _dot_nt(qd_ref[rows, sl], states[p].astype(BF16))
                dec = jnp.exp(ge_ref[ci * c:ci * c + 1, sl])
                states[p] = states[p] * dec + jnp.where(same_head, upd, 0.0)
            yield
        for p in range(N_PAIRS):
            s_ref[p] = states[p]

    _run_lockstep([intra(), inter()])
    acc = band[0]

    o = acc + os_ref[...]
    ms = _group_sum(o * o, e) * (1.0 / HEAD)
    og = og_ref[...].astype(F32)
    o_ref[...] = (o * lax.rsqrt(ms + NORM_EPS) * gain_ref[...] * (og * _sigmoid(og))).astype(o_ref.dtype)


def _hgrn(main, aux, lower, o_gain, bsz, seq, tm):
    nt = seq // tm
    ch = COL_HGRN // MIX
    row = lambda cidx: (lambda b, i: (b * nt + i, cidx))
    const = lambda b, i: (0, 0)
    return pl.pallas_call(
        _hgrn_kernel,
        grid=(bsz, nt),
        in_specs=[pl.BlockSpec((tm, MIX), row(ch)),
                  pl.BlockSpec((tm, MIX), row(ch + 1)),
                  pl.BlockSpec((tm, MIX), row(ch + 2)),
                  pl.BlockSpec((tm, MIX), row(0)),
                  pl.BlockSpec((1, MIX), const),
                  pl.BlockSpec((1, MIX), const),
                  pl.BlockSpec((MIX // 2, MIX // 2), const),
                  pl.BlockSpec((tm, tm), const)],
        out_specs=pl.BlockSpec((tm, MIX), lambda b, i: (b * nt + i, 0)),
        out_shape=jax.ShapeDtypeStruct((bsz * seq, MIX), BF16),
        scratch_shapes=[pltpu.VMEM((N_PAIRS, LANES, LANES), F32),
                        pltpu.VMEM((tm, MIX), BF16),
                        pltpu.VMEM((tm, MIX), BF16),
                        pltpu.VMEM((tm, MIX), BF16),
                        pltpu.VMEM((tm, MIX), F32),
                        pltpu.VMEM((tm, MIX), F32)],
        compiler_params=_cparams(("parallel", "arbitrary")),
        name="hgrn2",
    )(main, main, main, aux, lower, o_gain, _block_ones(MIX // 2), _block_tril(tm, HGRN_SUB))


def _rwkv_prep_kernel(*refs, has_vres):
    if has_vres:
        (p_ref, ph_ref, mu_ref, w0_ref, a0_ref, wa2_ref, g2_ref, kkk_ref, ka_ref, e_ref,
         misc_ref, misch_ref, vmu_ref, v0_ref, v2_ref, vf_ref,
         r_o, k_o, v_o, kk_o, a_o, lw_o, g_o) = refs
    else:
        (p_ref, ph_ref, mu_ref, w0_ref, a0_ref, wa2_ref, g2_ref, kkk_ref, ka_ref, e_ref,
         r_o, k_o, v_o, kk_o, a_o, lw_o, g_o) = refs
    tm = p_ref.shape[0]
    first = pl.program_id(1) == 0

    def token_shift(cur, halo_ref):
        prev_last = halo_ref[halo_ref.shape[0] - 1:, :].astype(F32)
        prev_last = jnp.where(first, 0.0, prev_last)
        rolled = pltpu.roll(cur, 1, 0)
        row = lax.broadcasted_iota(jnp.int32, cur.shape, 0)
        return jnp.where(row == 0, prev_last, rolled)

    p = p_ref[...].astype(F32)
    pm = p + (token_shift(p, ph_ref) - p) * mu_ref[...]
    r = pm[:, 0:MIX]
    k_raw = pm[:, MIX:2 * MIX]
    v = pm[:, 2 * MIX:3 * MIX]
    lora = pm[:, 3 * MIX:3 * MIX + LANES]
    g_lo = pm[:, 3 * MIX + LANES:3 * MIX + 2 * LANES]
    lane = lax.broadcasted_iota(jnp.int32, (tm, LANES), 1)
    z = jnp.where(lane < HEAD, jnp.tanh(lora), lora)
    wa = jnp.dot(z.astype(BF16), wa2_ref[...], preferred_element_type=F32)
    w_raw = _log_sigmoid(w0_ref[...] + wa[:, :MIX]) - 0.5
    lw_o[...] = -jnp.exp(w_raw)
    a = _sigmoid(a0_ref[...] + wa[:, MIX:])
    g_o[...] = jnp.dot(_sigmoid(g_lo).astype(BF16), g2_ref[...], preferred_element_type=F32).astype(BF16)
    if has_vres:
        m = misc_ref[...]
        ms = m + (token_shift(m, misch_ref) - m) * vmu_ref[...]
        mix = _sigmoid(v0_ref[...] + jnp.dot(ms.astype(BF16), v2_ref[...], preferred_element_type=F32))
        v = v + (vf_ref[...].astype(F32) - v) * mix
    kk = k_raw * kkk_ref[...]
    ss = _group_sum(kk * kk, e_ref[...])
    kk_o[...] = (kk * lax.rsqrt(jnp.maximum(ss, 1e-24))).astype(BF16)
    r_o[...] = r.astype(BF16)
    k_o[...] = (k_raw * (1.0 + (a - 1.0) * ka_ref[...])).astype(BF16)
    v_o[...] = v.astype(BF16)
    a_o[...] = a.astype(BF16)


def _rwkv_prep(main, aux, prm, v_first, bsz, seq, tm):
    nt = seq // tm
    has_vres = v_first is not None
    pw = 4 * MIX
    cp = COL_RWKV // pw
    halo = 16
    row = lambda cidx: (lambda b, i: (b * nt + i, cidx))
    const = lambda b, i: (0, 0)

    def halo_map(rows, cidx):
        per = tm // rows
        return lambda b, i: (jnp.maximum((b * nt + i) * per - 1, 0), cidx)

    in_specs = [pl.BlockSpec((tm, pw), row(cp)),
                pl.BlockSpec((halo, pw), halo_map(halo, cp)),
                pl.BlockSpec((1, pw), const),
                pl.BlockSpec((1, MIX), const),
                pl.BlockSpec((1, MIX), const),
                pl.BlockSpec((LANES, 2 * MIX), const),
                pl.BlockSpec((LANES, MIX), const),
                pl.BlockSpec((1, MIX), const),
                pl.BlockSpec((1, MIX), const),
                pl.BlockSpec((MIX // 2, MIX // 2), const)]
    args = [main, main, prm["mu"], prm["w0"], prm["a0"], prm["wa2"], prm["g2"], prm["k_k"], prm["k_a"],
            _block_ones(MIX // 2)]
    if has_vres:
        in_specs += [pl.BlockSpec((tm, LANES), row(MIX // LANES)),
                     pl.BlockSpec((8, LANES), halo_map(8, MIX // LANES)),
                     pl.BlockSpec((1, LANES), const),
                     pl.BlockSpec((1, MIX), const),
                     pl.BlockSpec((LANES, MIX), const),
                     pl.BlockSpec((tm, MIX), row(0))]
        args += [aux, aux, prm["vmu"], prm["v0"], prm["v2"], v_first]
    ospec = pl.BlockSpec((tm, MIX), lambda b, i: (b * nt + i, 0))
    o16 = jax.ShapeDtypeStruct((bsz * seq, MIX), BF16)
    o32 = jax.ShapeDtypeStruct((bsz * seq, MIX), F32)
    return pl.pallas_call(
        functools.partial(_rwkv_prep_kernel, has_vres=has_vres),
        grid=(bsz, nt),
        in_specs=in_specs,
        out_specs=[ospec] * 7,
        out_shape=[o16, o16, o16, o16, o16, o32, o16],
        compiler_params=_cparams(("parallel", "arbitrary")),
        name="rwkv_prep",
    )(*args)


def _rwkv_scan_kernel(r_ref, k_ref, v_ref, kk_ref, a_ref, lw_ref, g_ref, rk_ref, lnw_ref, lnb_ref,
                      tri_ref, e_ref, o_ref,
                      s_ref, kq_s, rq_s, bk_s, kx_s, nb_s, kd_s, dec_s, tkq_s, rqe_s, w_s, ol_s, oo_s):
    tm = r_ref.shape[0]
    c = RWKV_CHUNK
    n = tm // c

    @pl.when(pl.program_id(1) == 0)
    def _():
        s_ref[...] = jnp.zeros_like(s_ref)

    lw = lw_ref[...]
    gam = _exact_dot(tri_ref[...], lw)
    g_last = jnp.broadcast_to(gam.reshape(n, c, MIX)[:, c - 1:c, :], (n, c, MIX)).reshape(tm, MIX)
    r = r_ref[...].astype(F32)
    k = k_ref[...].astype(F32)
    kk = kk_ref[...].astype(F32)
    b = a_ref[...].astype(F32) * kk
    e_inv = jnp.exp(-gam)
    e_dec = jnp.exp(g_last - gam)
    kq_s[...] = (kk * jnp.exp(gam - lw)).astype(BF16)
    rq_s[...] = (r * jnp.exp(gam)).astype(BF16)
    bk_s[...] = (b * e_inv).astype(BF16)
    kx_s[...] = (k * e_inv).astype(BF16)
    nb_s[...] = (-(b * e_dec)).astype(BF16)
    kd_s[...] = (k * e_dec).astype(BF16)
    dec_s[...] = jnp.exp(g_last)

    lane = lax.broadcasted_iota(jnp.int32, (c, LANES), 1)
    t_idx = lax.broadcasted_iota(jnp.int32, (c, LANES), 0)
    s_idx = lane % HEAD
    left = lane < HEAD
    strict = t_idx > s_idx
    incl = t_idx >= s_idx
    eye = jnp.where(t_idx == s_idx, 1.0, 0.0)
    lane_r = lax.broadcasted_iota(jnp.int32, (LANES, LANES), 0) // HEAD
    lane_c = lax.broadcasted_iota(jnp.int32, (LANES, LANES), 1) // HEAD
    same_head = lane_r == lane_c

    def halves(x):
        zero = jnp.zeros_like(x)
        return jnp.where(left, x, zero), jnp.where(left, zero, x)

    def bdiag(x):
        x0, x1 = halves(x.astype(BF16))
        return jnp.concatenate([x0, x1], axis=0)

    def mm(a, rhs):
        return jnp.dot(a.astype(BF16), rhs, preferred_element_type=F32)

    group = 2

    def local_chain(start, p):
        rows = pl.ds(start, c)
        sl = slice(p * LANES, (p + 1) * LANES)
        kq, rq = kq_s[rows, sl], rq_s[rows, sl]
        lhs = jnp.concatenate([kq, rq], axis=0)
        rhs = jnp.concatenate(halves(bk_s[rows, sl]) + halves(kx_s[rows, sl]), axis=0)
        aa = _dot_nt(lhs, rhs)
        yield
        a_ab = jnp.where(strict, aa[:c, :LANES], 0.0)
        a_ak = jnp.where(strict, aa[:c, LANES:], 0.0)
        a_rb = jnp.where(incl, aa[c:, :LANES], 0.0)
        a_rk = jnp.where(incl, aa[c:, LANES:], 0.0)
        yo = mm(jnp.concatenate([a_ak, a_rk], axis=0), bdiag(v_ref[rows, sl]))
        yield
        pw = -a_ab
        tinv = eye + pw
        pw = mm(pw, bdiag(pw))
        yield
        for _ in range(4):
            res = mm(pw, jnp.concatenate([bdiag(tinv), bdiag(pw)], axis=1))
            yield
            tinv = tinv + res[:, :LANES]
            pw = res[:, LANES:]
        tinv = tinv + mm(pw, bdiag(tinv))
        yield
        tw = mm(tinv, jnp.concatenate([bdiag(kq), bdiag(yo[:c])], axis=1))
        yield
        ar = mm(a_rb, jnp.concatenate([bdiag(tw[:, :LANES]), bdiag(tw[:, LANES:])], axis=1))
        yield
        tkq_s[rows, sl] = tw[:, :LANES].astype(BF16)
        w_s[rows, sl] = tw[:, LANES:]
        rqe_s[rows, sl] = (rq.astype(F32) - ar[:, :LANES]).astype(BF16)
        ol_s[rows, sl] = yo[c:] - ar[:, LANES:]

    def seq_group(gi):
        for u in range(group):
            start = pl.multiple_of((gi * group + u) * c, c)
            rows = pl.ds(start, c)
            states = [s_ref[p] for p in range(N_PAIRS)]
            xs = []
            for p in range(N_PAIRS):
                sl = slice(p * LANES, (p + 1) * LANES)
                lhs = jnp.concatenate([tkq_s[rows, sl], rqe_s[rows, sl]], axis=0)
                xs.append(_dot_nt(lhs, states[p].astype(BF16)))
            yield
            upds = []
            for p in range(N_PAIRS):
                sl = slice(p * LANES, (p + 1) * LANES)
                uu = xs[p][:c] + w_s[rows, sl]
                oo_s[rows, sl] = xs[p][c:] + ol_s[rows, sl]
                wv = jnp.concatenate([uu.astype(BF16), v_ref[rows, sl]], axis=0)
                wk = jnp.concatenate([nb_s[rows, sl], kd_s[rows, sl]], axis=0)
                upds.append(_dot_tn(wv, wk))
            yield
            for p in range(N_PAIRS):
                sl = slice(p * LANES, (p + 1) * LANES)
                s_ref[p] = states[p] * dec_s[pl.ds(start, 1), sl] + jnp.where(same_head, upds[p], 0.0)

    def local_group(gi):
        chains = []
        for u in range(group):
            start = pl.multiple_of((gi * group + u) * c, c)
            chains += [local_chain(start, p) for p in range(N_PAIRS)]
        return chains

    n_groups = n // group
    _run_lockstep(local_group(0))

    def body(gi, carry):
        _run_lockstep(local_group(gi) + [seq_group(gi - 1)])
        return carry

    lax.fori_loop(1, n_groups, body, 0)
    _run_lockstep([seq_group(n_groups - 1)])

    e = e_ref[...]
    o = oo_s[...]
    v = v_ref[...].astype(F32)
    mean = _group_sum2(o, e) * (1.0 / HEAD)
    d = o - mean
    var = _group_sum2(d * d, e) * (1.0 / HEAD)
    bonus = _group_sum2(r * k * rk_ref[...], e)
    out = d * lax.rsqrt(var + RWKV_LN_EPS) * lnw_ref[...] + lnb_ref[...] + bonus * v
    o_ref[...] = (out * g_ref[...].astype(F32)).astype(o_ref.dtype)


def _rwkv_scan(r, k, v, kk, a, lw, g, r_k, ln_w, ln_b, bsz, seq, tm):
    nt = seq // tm
    spec = pl.BlockSpec((tm, MIX), lambda b, i: (b * nt + i, 0))
    const = lambda b, i: (0, 0)
    vec = pl.BlockSpec((1, MIX), const)
    s16 = pltpu.VMEM((tm, MIX), BF16)
    s32 = pltpu.VMEM((tm, MIX), F32)
    return pl.pallas_call(
        _rwkv_scan_kernel,
        grid=(bsz, nt),
        in_specs=[spec] * 7 + [vec, vec, vec,
                               pl.BlockSpec((tm, tm), const),
                               pl.BlockSpec((MIX // 2, MIX // 2), const)],
        out_specs=spec,
        out_shape=jax.ShapeDtypeStruct((bsz * seq, MIX), BF16),
        scratch_shapes=[pltpu.VMEM((N_PAIRS, LANES, LANES), F32),
                        s16, s16, s16, s16, s16, s16, s32, s16, s16, s32, s32, s32],
        compiler_params=_cparams(("parallel", "arbitrary")),
        name="rwkv_scan",
    )(r, k, v, kk, a, lw, g, r_k, ln_w, ln_b, _block_tril(tm, RWKV_CHUNK), _block_ones(MIX // 2))


def _merge_kernel(yf_ref, yh_ref, yr_ref, gf_ref, gh_ref, gr_ref, x_ref, gt_ref, wb_ref, wo_ref, o_ref):
    def branch(y_ref, g_ref, i):
        return _sigmoid(g_ref[...].astype(F32)) * jnp.dot(y_ref[...], wb_ref[i], preferred_element_type=F32)

    merged = branch(yf_ref, gf_ref, 0) + branch(yh_ref, gh_ref, 1) + branch(yr_ref, gr_ref, 2)
    o_ref[...] = x_ref[...] + gt_ref[0] * jnp.dot(merged.astype(BF16), wo_ref[...], preferred_element_type=F32)


def _merge(y_fox, y_hgrn, y_rwkv, main, x, gt, w_branch, w_out, seq, tm):
    t, d = x.shape
    tps = seq // tm
    yspec = pl.BlockSpec((tm, MIX), lambda i: (i, 0))
    gspec = lambda cidx: pl.BlockSpec((tm, d), lambda i: (i, cidx))
    return pl.pallas_call(
        _merge_kernel,
        grid=(t // tm,),
        in_specs=[yspec, yspec, yspec, gspec(0), gspec(1), gspec(2),
                  pl.BlockSpec((tm, d), lambda i: (i, 0)),
                  pl.BlockSpec((1, 1, d), lambda i: (i // tps, 0, 0)),
                  pl.BlockSpec((3, MIX, d), lambda i: (0, 0, 0)),
                  pl.BlockSpec((d, d), lambda i: (0, 0))],
        out_specs=pl.BlockSpec((tm, d), lambda i: (i, 0)),
        out_shape=jax.ShapeDtypeStruct((t, d), F32),
        compiler_params=_cparams(("parallel",)),
        name="merge_out",
    )(y_fox, y_hgrn, y_rwkv, main, main, main, x, gt, w_branch, w_out)


FFN_HALO = 16


FFN_TF = 256


def _ffn_body(x_ref, xh_ref, sc_ref, sh_ref, gt_ref, wup_ref, cw_ref, cb_ref, wd_ref, o_ref, h_ref, act_ref,
              *, tps):
    def normed(x):
        ms = jnp.mean(x * x, axis=-1, keepdims=True)
        return x * lax.rsqrt(ms + NORM_EPS) * sc_ref[0] + sh_ref[0]

    seq_start = pl.program_id(0) % tps == 0
    h_ref[0:FFN_HALO, :] = jnp.where(seq_start, 0.0, normed(xh_ref[...])).astype(BF16)
    h_ref[FFN_HALO:, :] = normed(x_ref[...]).astype(BF16)

    def up_proj(col):
        return jnp.dot(h_ref[...], wup_ref[:, col:col + FFN_TF], preferred_element_type=F32)

    def conv(u, col):
        cw = cw_ref[:, col:col + FFN_TF]
        y = cb_ref[:, col:col + FFN_TF] + cw[0:1] * pltpu.roll(u, 2, 0) + cw[1:2] * pltpu.roll(u, 1, 0) + cw[2:3] * u
        return y[FFN_HALO:]

    cols = [base + j * FFN_TF for j in range(D_FF // FFN_TF) for base in (0, D_FF)]
    u_next = up_proj(cols[0])
    val = None
    for i, col in enumerate(cols):
        u = u_next
        if i + 1 < len(cols):
            u_next = up_proj(cols[i + 1])
        y = conv(u, col)
        if i % 2 == 0:
            val = y
        else:
            act_ref[:, col - D_FF:col - D_FF + FFN_TF] = (y * _sigmoid(y) * val).astype(BF16)

    down = jnp.dot(act_ref[...], wd_ref[...], preferred_element_type=F32)
    o_ref[...] = x_ref[...] + gt_ref[0] * down


def _ffn(x, scale, shift, gt, w_up, conv_w, conv_b, w_down, seq, tm):
    t, d = x.shape
    tps = seq // tm
    per = tm // FFN_HALO
    bmap = lambda i: (i // tps, 0, 0)
    const = lambda i: (0, 0)
    resident = dict(pipeline_mode=pl.Buffered(1))
    return pl.pallas_call(
        functools.partial(_ffn_body, tps=tps),
        grid=(t // tm,),
        in_specs=[pl.BlockSpec((tm, d), lambda i: (i, 0)),
                  pl.BlockSpec((FFN_HALO, d), lambda i: (jnp.maximum(i * per - 1, 0), 0)),
                  pl.BlockSpec((1, 1, d), bmap),
                  pl.BlockSpec((1, 1, d), bmap),
                  pl.BlockSpec((1, 1, d), bmap),
                  pl.BlockSpec((d, 2 * D_FF), const, **resident),
                  pl.BlockSpec((3, 2 * D_FF), const),
                  pl.BlockSpec((1, 2 * D_FF), const),
                  pl.BlockSpec((D_FF, d), const, **resident)],
        out_specs=pl.BlockSpec((tm, d), lambda i: (i, 0)),
        out_shape=jax.ShapeDtypeStruct((t, d), F32),
        scratch_shapes=[pltpu.VMEM((tm + FFN_HALO, d), BF16), pltpu.VMEM((tm, D_FF), BF16)],
        compiler_params=_cparams(("parallel",)),
        name="conv_mlp",
    )(x, x, scale, shift, gt, w_up, conv_w, conv_b, w_down)


def _pack_in_proj(w_in, vres_down):
    depth, d, _ = w_in.shape
    fox, hgrn, rwkv, gate = 0, 1544, 3592, 5384
    z = lambda n: jnp.zeros((depth, d, n), w_in.dtype)
    main = jnp.concatenate([
        w_in[..., gate:gate + 3072],
        w_in[..., fox:fox + 1536],
        w_in[..., hgrn:hgrn + 512], w_in[..., hgrn + 1024:hgrn + 2048],
        w_in[..., rwkv:rwkv + 1792],
        z(N_MAIN - 7936)], axis=-1).astype(BF16)
    vres = jnp.concatenate([jnp.zeros((1, d, VRES_RANK), w_in.dtype), vres_down], axis=0)
    aux = jnp.concatenate([
        w_in[..., hgrn + 512:hgrn + 1024],
        w_in[..., fox + 1536:fox + 1544], z(AUX_VRES_LANE - 8),
        vres, z(LANES - AUX_VRES_LANE - VRES_RANK)], axis=-1).astype(BF16)
    return main, aux


def _tile_heads(v):
    return jnp.tile(v, N_HEADS).reshape(1, MIX)


def _forward(x, c, w_ada, b_ada, norm1_g, norm2_g, w_in, fox_b_f, fox_q_gain, fox_k_gain,
             hgrn_lb, hgrn_o_gain, rwkv_mu, rwkv_w0, rwkv_w2, rwkv_a0, rwkv_a2, rwkv_g2,
             rwkv_k_k, rwkv_k_a, rwkv_r_k, rwkv_ln_w, rwkv_ln_b, rwkv_vres_down, rwkv_vres_mu,
             rwkv_v0, rwkv_v2, w_branch, w_out, w_up, conv_w, conv_b, w_down, *, tiles):
    bsz, seq, d = x.shape
    depth = w_in.shape[0]
    t = bsz * seq

    lb_prob = jax.nn.softmax(hgrn_lb.astype(F32), axis=0)
    hgrn_lower = jnp.cumsum(lb_prob, axis=0) - lb_prob[0]

    mod = _ada_mod(c, w_ada, b_ada)
    w_main, w_aux = _pack_in_proj(w_in, rwkv_vres_down)
    w_branch16, w_out16 = w_branch.astype(BF16), w_out.astype(BF16)
    w_up16, w_down16 = w_up.astype(BF16), w_down.astype(BF16)

    xf = x.reshape(t, d)
    v_first = None
    for l in range(depth):
        sh1, sc1, gt1, sh2, sc2, gt2 = [m.reshape(bsz, 1, d) for m in jnp.split(mod[l], 6, axis=-1)]
        scale1 = norm1_g[l] * (1.0 + sc1)
        main, aux = _in_proj(xf, scale1, sh1, w_main[l], w_aux[l], seq, tiles["proj_tm"], tiles["proj_tn"])

        b_f = jnp.zeros((1, LANES), F32).at[0, AUX_FF_LANE:AUX_FF_LANE + N_HEADS].set(fox_b_f[l])
        qh, kh, vh = _fox_prep(main, aux, _tile_heads(fox_q_gain[l]) * HEAD ** -0.5, _tile_heads(fox_k_gain[l]),
                               b_f, bsz, seq, tiles["fox_prep_tm"])
        y_fox = _fox_attn(qh, kh, vh, tiles["fox_tq"], tiles["fox_sub"], tiles["fox_tk"])

        y_hgrn = _hgrn(main, aux, hgrn_lower[l].reshape(1, MIX), _tile_heads(hgrn_o_gain[l]), bsz, seq,
                       tiles["hgrn_tm"])

        wa2 = jnp.zeros((LANES, 2 * MIX), F32)
        wa2 = wa2.at[:HEAD, :MIX].set(rwkv_w2[l]).at[HEAD:, MIX:].set(rwkv_a2[l])
        prm = {
            "mu": jnp.concatenate([rwkv_mu[l], jnp.zeros((4 * MIX - rwkv_mu.shape[1],), F32)]).reshape(1, 4 * MIX),
            "w0": rwkv_w0[l].reshape(1, MIX), "a0": rwkv_a0[l].reshape(1, MIX),
            "wa2": wa2.astype(BF16), "g2": rwkv_g2[l].astype(BF16),
            "k_k": rwkv_k_k[l].reshape(1, MIX), "k_a": rwkv_k_a[l].reshape(1, MIX),
        }
        if l > 0:
            vmu = jnp.zeros((1, LANES), F32).at[0, AUX_VRES_LANE:AUX_VRES_LANE + VRES_RANK].set(rwkv_vres_mu[l - 1])
            v2 = jnp.zeros((LANES, MIX), F32).at[AUX_VRES_LANE:AUX_VRES_LANE + VRES_RANK].set(rwkv_v2[l - 1])
            prm.update(vmu=vmu, v0=rwkv_v0[l - 1].reshape(1, MIX), v2=v2.astype(BF16))
        r, k, v, kk, a, lw, g = _rwkv_prep(main, aux, prm, v_first, bsz, seq, tiles["rwkv_prep_tm"])
        if l == 0:
            v_first = v
        y_rwkv = _rwkv_scan(r, k, v, kk, a, lw, g, rwkv_r_k[l].reshape(1, MIX), rwkv_ln_w[l].reshape(1, MIX),
                            rwkv_ln_b[l].reshape(1, MIX), bsz, seq, tiles["rwkv_tm"])

        xf = _merge(y_fox, y_hgrn, y_rwkv, main, xf, gt1, w_branch16[l], w_out16[l], seq, tiles["merge_tm"])

        scale2 = norm2_g[l] * (1.0 + sc2)
        xf = _ffn(xf, scale2, sh2, gt2, w_up16[l], conv_w[l], conv_b[l].reshape(1, -1), w_down16[l], seq,
                  tiles["ffn_tm"])
    return xf.reshape(bsz, seq, d)


def _tiles_for(seq):
    cap = lambda n: min(n, seq)
    return dict(proj_tm=cap(1024), proj_tn=1024, fox_prep_tm=cap(512), fox_tq=cap(1024), fox_sub=256,
                fox_tk=cap(512), hgrn_tm=cap(256), rwkv_prep_tm=cap(256), rwkv_tm=cap(512), merge_tm=cap(512),
                ffn_tm=cap(512))


def kernel(x, c, w_ada, b_ada, norm1_g, norm2_g, w_in, fox_b_f, fox_q_gain, fox_k_gain, hgrn_lb, hgrn_o_gain, rwkv_mu, rwkv_w0, rwkv_w2, rwkv_a0, rwkv_a2, rwkv_g2, rwkv_k_k, rwkv_k_a, rwkv_r_k, rwkv_ln_w, rwkv_ln_b, rwkv_vres_down, rwkv_vres_mu, rwkv_v0, rwkv_v2, w_branch, w_out, w_up, conv_w, conv_b, w_down):
    return _forward(x, c, w_ada, b_ada, norm1_g, norm2_g, w_in, fox_b_f, fox_q_gain, fox_k_gain,
                    hgrn_lb, hgrn_o_gain, rwkv_mu, rwkv_w0, rwkv_w2, rwkv_a0, rwkv_a2, rwkv_g2,
                    rwkv_k_k, rwkv_k_a, rwkv_r_k, rwkv_ln_w, rwkv_ln_b, rwkv_vres_down, rwkv_vres_mu,
                    rwkv_v0, rwkv_v2, w_branch, w_out, w_up, conv_w, conv_b, w_down,
                    tiles=_tiles_for(x.shape[1]))
```

```python
import functools

import jax
import jax.numpy as jnp
import numpy as np
from jax import lax
from jax.experimental import pallas as pl
from jax.experimental.pallas import tpu as pltpu

F32 = jnp.float32
BF16 = jnp.bfloat16

D_MODEL = 1024
MIX = 512
HEAD = 64
N_HEADS = MIX // HEAD
N_PAIRS = N_HEADS // 2
LANES = 128
D_FF = 2816
NORM_EPS = 1e-6
RWKV_LN_EPS = 64e-5
MASK_VALUE = -1e30
LOG_FLOOR = 1e-30
LOG2E = 1.4426950408889634
VRES_RANK = 32

HGRN_SUB = 16
RWKV_CHUNK = 64

COL_GATE = 0
COL_FOX = 3072
COL_HGRN = 4608
COL_RWKV = 6144
N_MAIN = 8192
N_AUX = 640
AUX_FF_LANE = 0
AUX_VRES_LANE = 32

VMEM_LIMIT = 56 * 1024 * 1024


def _cparams(sem):
    return pltpu.CompilerParams(dimension_semantics=sem, vmem_limit_bytes=VMEM_LIMIT)


def _split3(x):
    hi = x.astype(BF16)
    r1 = x - hi.astype(F32)
    mid = r1.astype(BF16)
    lo = (r1 - mid.astype(F32)).astype(BF16)
    return hi, mid, lo


def _exact_dot(a01, x):
    hi, mid, lo = _split3(x)
    d = lambda b: jnp.dot(a01, b, preferred_element_type=F32)
    return d(hi) + d(mid) + d(lo)


def _group_sum(x, e):
    w = e.shape[0]
    xb = x.astype(BF16)
    if x.shape[1] == w:
        return jnp.dot(xb, e, preferred_element_type=F32)
    parts = [jnp.dot(xb[:, i:i + w], e, preferred_element_type=F32) for i in range(0, x.shape[1], w)]
    return jnp.concatenate(parts, axis=1)


def _group_sum2(x, e):
    hi = x.astype(BF16)
    lo = x - hi.astype(F32)
    return _group_sum(hi, e) + _group_sum(lo, e)


def _sigmoid(x):
    return 1.0 / (1.0 + jnp.exp(-x))


def _log_sigmoid(x):
    return jnp.minimum(x, 0.0) - jnp.log1p(jnp.exp(-jnp.abs(x)))


def _dot_nt(a, b):
    return lax.dot_general(a, b, (((1,), (1,)), ((), ())), preferred_element_type=F32)


def _dot_tn(a, b):
    return lax.dot_general(a, b, (((0,), (0,)), ((), ())), preferred_element_type=F32)


def _run_lockstep(gens):
    live = list(gens)
    while live:
        nxt = []
        for g in live:
            try:
                next(g)
                nxt.append(g)
            except StopIteration:
                pass
        live = nxt


def _block_ones(width):
    g = np.arange(width) // HEAD
    return jnp.asarray(g[:, None] == g[None, :], dtype=BF16)


def _block_tril(n, blk):
    i = np.arange(n)
    m = (i[:, None] >= i[None, :]) & ((i[:, None] // blk) == (i[None, :] // blk))
    return jnp.asarray(m, dtype=BF16)


def _ada_kernel(c_ref, w_ref, b_ref, o_ref):
    c = c_ref[...]
    cond = c * _sigmoid(c)
    o_ref[0] = jnp.dot(cond.astype(BF16), w_ref[0].astype(BF16), preferred_element_type=F32) + b_ref[0]


def _ada_mod(c, w_ada, b_ada, tn=1536):
    depth, d, n = w_ada.shape
    bsz = c.shape[0]
    return pl.pallas_call(
        _ada_kernel,
        grid=(depth, n // tn),
        in_specs=[pl.BlockSpec((bsz, d), lambda l, j: (0, 0)),
                  pl.BlockSpec((1, d, tn), lambda l, j: (l, 0, j)),
                  pl.BlockSpec((1, 1, tn), lambda l, j: (l, 0, j))],
        out_specs=pl.BlockSpec((1, bsz, tn), lambda l, j: (l, 0, j)),
        out_shape=jax.ShapeDtypeStruct((depth, bsz, n), F32),
        compiler_params=_cparams(("parallel", "parallel")),
        name="ada_mod",
    )(c, w_ada, b_ada.reshape(depth, 1, n))


def _in_proj_kernel(x_ref, sc_ref, sh_ref, w_ref, wa_ref, o_ref, oa_ref, h_ref, *, n_main):
    j = pl.program_id(1)

    @pl.when(j == 0)
    def _():
        x = x_ref[...]
        ms = jnp.mean(x * x, axis=-1, keepdims=True)
        h_ref[...] = (x * lax.rsqrt(ms + NORM_EPS) * sc_ref[0] + sh_ref[0]).astype(BF16)

    @pl.when(j < n_main)
    def _():
        o_ref[...] = jnp.dot(h_ref[...], w_ref[...], preferred_element_type=F32).astype(o_ref.dtype)

    @pl.when(j == n_main)
    def _():
        oa_ref[...] = jnp.dot(h_ref[...], wa_ref[...], preferred_element_type=F32)


def _in_proj(x, scale, shift, w, w_aux, seq, tm, tn):
    t, d = x.shape
    n = w.shape[1]
    n_aux = w_aux.shape[1]
    n_main = n // tn
    tps = seq // tm
    col = lambda i, j: (0, jnp.minimum(j, n_main - 1))
    return pl.pallas_call(
        functools.partial(_in_proj_kernel, n_main=n_main),
        grid=(t // tm, n_main + 1),
        in_specs=[pl.BlockSpec((tm, d), lambda i, j: (i, 0)),
                  pl.BlockSpec((1, 1, d), lambda i, j: (i // tps, 0, 0)),
                  pl.BlockSpec((1, 1, d), lambda i, j: (i // tps, 0, 0)),
                  pl.BlockSpec((d, tn), col),
                  pl.BlockSpec((d, n_aux), lambda i, j: (0, 0))],
        out_specs=[pl.BlockSpec((tm, tn), lambda i, j: (i, jnp.minimum(j, n_main - 1))),
                   pl.BlockSpec((tm, n_aux), lambda i, j: (i, 0))],
        out_shape=[jax.ShapeDtypeStruct((t, n), BF16), jax.ShapeDtypeStruct((t, n_aux), F32)],
        scratch_shapes=[pltpu.VMEM((tm, d), BF16)],
        compiler_params=_cparams(("parallel", "arbitrary")),
        name="in_proj",
    )(x, scale, shift, w, w_aux)


def _fox_prep_kernel(q_ref, k_ref, v_ref, misc_ref, qg_ref, kg_ref, bf_ref, e_ref, tri_ref, sel_ref,
                     qo_ref, ko_ref, vo_ref, carry_ref):
    tm = q_ref.shape[0]

    @pl.when(pl.program_id(1) == 0)
    def _():
        carry_ref[...] = jnp.zeros_like(carry_ref)

    e = e_ref[...]

    def qk_norm(ref, gain):
        xx = ref[...].astype(F32)
        ss = _group_sum(xx * xx, e)
        return xx * lax.rsqrt(ss * (1.0 / HEAD) + NORM_EPS) * gain

    qn = qk_norm(q_ref, qg_ref[...])
    kn = qk_norm(k_ref, kg_ref[...])
    vv = v_ref[...].astype(F32)

    lf = _log_sigmoid(misc_ref[...] + bf_ref[...])
    cum = _exact_dot(tri_ref[...], lf) + carry_ref[...]
    carry_ref[...] = cum[tm - 1:tm, :]
    hi, mid, lo = [s.astype(F32) for s in _split3(cum)]
    lane = lax.broadcasted_iota(jnp.int32, (tm, LANES), 1)
    fcat = jnp.where(lane < 8, hi, jnp.where(lane < 16, pltpu.roll(mid, 8, 1),
                     jnp.where(lane < 24, pltpu.roll(lo, 16, 1), jnp.where(lane == 24, 1.0, 0.0))))
    ext = jnp.dot(fcat.astype(BF16), sel_ref[...], preferred_element_type=F32)

    for p in range(N_PAIRS):
        sl = slice(p * LANES, (p + 1) * LANES)
        qp, kp, vp = qn[:, sl], kn[:, sl], vv[:, sl]
        for par in range(2):
            h = 2 * p + par
            base = HEAD if par == 0 else 0
            keep = (lane < HEAD) if par == 0 else (lane >= HEAD)
            qo_ref[0, h] = jnp.where(keep, qp, ext[:, 2 * h * LANES:(2 * h + 1) * LANES]).T.astype(BF16)
            ko_ref[0, h] = jnp.where(keep, kp, ext[:, (2 * h + 1) * LANES:(2 * h + 2) * LANES]).astype(BF16)
            vo_ref[0, h] = jnp.where(keep, vp, jnp.where(lane == base, 1.0, 0.0)).T.astype(BF16)


def _fox_sel():
    sel = np.zeros((LANES, N_HEADS, 2, LANES), np.float32)
    for h in range(N_HEADS):
        base = HEAD if h % 2 == 0 else 0
        for i in range(3):
            sel[8 * i + h, h, 0, base + i] = 1.0
            sel[24, h, 0, base + 3 + i] = 1.0
            sel[24, h, 1, base + i] = 1.0
            sel[8 * i + h, h, 1, base + 3 + i] = -1.0
    return jnp.asarray(sel.reshape(LANES, N_HEADS * 2 * LANES), dtype=BF16)


def _fox_prep(main, aux, q_gain, k_gain, b_f, bsz, seq, tm):
    nt = seq // tm
    cq = COL_FOX // MIX
    out = jax.ShapeDtypeStruct((bsz, N_HEADS, seq, LANES), BF16)
    out_t = jax.ShapeDtypeStruct((bsz, N_HEADS, LANES, seq), BF16)
    ospec = pl.BlockSpec((1, N_HEADS, tm, LANES), lambda b, i: (b, 0, i, 0))
    ospec_t = pl.BlockSpec((1, N_HEADS, LANES, tm), lambda b, i: (b, 0, 0, i))
    row = lambda c: (lambda b, i: (b * nt + i, c))
    const = lambda b, i: (0, 0)
    return pl.pallas_call(
        _fox_prep_kernel,
        grid=(bsz, nt),
        in_specs=[pl.BlockSpec((tm, MIX), row(cq)),
                  pl.BlockSpec((tm, MIX), row(cq + 1)),
                  pl.BlockSpec((tm, MIX), row(cq + 2)),
                  pl.BlockSpec((tm, LANES), row(MIX // LANES)),
                  pl.BlockSpec((1, MIX), const),
                  pl.BlockSpec((1, MIX), const),
                  pl.BlockSpec((1, LANES), const),
                  pl.BlockSpec((MIX // 2, MIX // 2), const),
                  pl.BlockSpec((tm, tm), const),
                  pl.BlockSpec((LANES, N_HEADS * 2 * LANES), const)],
        out_specs=[ospec_t, ospec, ospec_t],
        out_shape=[out_t, out, out_t],
        scratch_shapes=[pltpu.VMEM((1, LANES), F32)],
        compiler_params=_cparams(("parallel", "arbitrary")),
        name="fox_prep",
    )(main, main, main, aux, q_gain, k_gain, b_f, _block_ones(MIX // 2), _block_tril(tm, tm), _fox_sel())


def _fox_attn_kernel(qt_ref, k_ref, vt_ref, o_ref, *, tq, sub, tk):
    qi = pl.program_id(2)
    n_sub = tq // sub
    kpt = tq // tk
    chains = [(hh, r) for hh in range(2) for r in range(n_sub)]

    def step(j, carry, key_off):
        new = list(carry)
        start = pl.multiple_of(j * tk, tk)

        def chain(idx, hh, r):
            q_lo = r * sub
            nk = tk if key_off is None else min(q_lo + sub - key_off, tk)
            if nk <= 0:
                return
            m, acc = carry[idx]
            qt = qt_ref[0, hh, :, q_lo:q_lo + sub]
            kb = k_ref[0, hh, pl.ds(start, nk), :]
            vtb = vt_ref[0, hh, :, pl.ds(start, nk)]
            st = jnp.dot(kb, qt, preferred_element_type=F32)
            yield
            if key_off is not None and key_off + nk > q_lo:
                kpos = lax.broadcasted_iota(jnp.int32, (nk, sub), 0) + key_off
                qpos = lax.broadcasted_iota(jnp.int32, (nk, sub), 1) + q_lo
                st = jnp.where(kpos <= qpos, st, MASK_VALUE)
            m_new = jnp.maximum(m, jnp.max(st, axis=0, keepdims=True))
            p = jnp.exp(st - m_new).astype(BF16)
            yield
            pv = jnp.dot(vtb, p, preferred_element_type=F32)
            yield
            new[idx] = (m_new, jnp.exp(m - m_new) * acc + pv)

        _run_lockstep([chain(i, hh, r) for i, (hh, r) in enumerate(chains)])
        return tuple(new)

    init = tuple((jnp.full((1, sub), MASK_VALUE, F32), jnp.zeros((LANES, sub), F32)) for _ in chains)
    carry = lax.fori_loop(0, qi * kpt, lambda j, c: step(j, c, None), init)
    for jd in range(kpt):
        carry = step(qi * kpt + jd, carry, jd * tk)
    row = lax.broadcasted_iota(jnp.int32, (LANES, tq), 0)
    accs = [jnp.concatenate([carry[hh * n_sub + r][1] for r in range(n_sub)], axis=1) for hh in range(2)]
    ot = jnp.where(row < HEAD, accs[0] / accs[0][HEAD:HEAD + 1, :], accs[1] / accs[1][0:1, :])
    o_ref[...] = ot.T.astype(o_ref.dtype)


def _fox_attn(qt, kh, vt, tq, sub, tk):
    bsz, _, seq, _ = kh.shape
    nq = seq // tq
    return pl.pallas_call(
        functools.partial(_fox_attn_kernel, tq=tq, sub=sub, tk=tk),
        grid=(bsz, N_PAIRS, nq),
        in_specs=[pl.BlockSpec((1, 2, LANES, tq), lambda b, p, i: (b, p, 0, i)),
                  pl.BlockSpec((1, 2, seq, LANES), lambda b, p, i: (b, p, 0, 0)),
                  pl.BlockSpec((1, 2, LANES, seq), lambda b, p, i: (b, p, 0, 0))],
        out_specs=pl.BlockSpec((tq, LANES), lambda b, p, i: (b * nq + i, p)),
        out_shape=jax.ShapeDtypeStruct((bsz * seq, MIX), BF16),
        compiler_params=_cparams(("parallel", "parallel", "arbitrary")),
        name="fox_attn",
    )(qt, kh, vt)


def _hgrn_kernel(q_ref, i_ref, og_ref, f_ref, lb_ref, gain_ref, e_ref, tri_ref, o_ref,
                 s_ref, qd_ref, kd_ref, vb_ref, ge_ref, os_ref):
    tm = q_ref.shape[0]
    c = HGRN_SUB
    n = tm // c

    @pl.when(pl.program_id(1) == 0)
    def _():
        s_ref[...] = jnp.zeros_like(s_ref)

    lb = lb_ref[...]
    gate = lb + (1.0 - lb) * _sigmoid(f_ref[...])
    ell = jnp.log(jnp.maximum(gate, LOG_FLOOR))
    kx = 1.0 - gate
    qraw = q_ref[...].astype(F32)
    q = qraw * _sigmoid(qraw)
    v = i_ref[...].astype(F32)
    g = _exact_dot(tri_ref[...], ell)

    def row_of_chunk(a, j):
        a3 = a.reshape(n, c, MIX)
        return jnp.broadcast_to(a3[:, j:j + 1, :], (n, c, MIX)).reshape(tm, MIX)

    g_end = row_of_chunk(g, c - 1)
    qd_ref[...] = (q * jnp.exp(g)).astype(BF16)
    kd_ref[...] = (kx * jnp.exp(g_end - g)).astype(BF16)
    vb_ref[...] = v.astype(BF16)
    ge_ref[...] = g_end

    e = e_ref[...]
    pos = lax.broadcasted_iota(jnp.int32, (tm, MIX), 0) % c
    lane_r = lax.broadcasted_iota(jnp.int32, (LANES, LANES), 0) // HEAD
    lane_c = lax.broadcasted_iota(jnp.int32, (LANES, LANES), 1) // HEAD
    same_head = lane_r == lane_c
    band = []

    def intra():
        g2 = g * LOG2E
        qb = q.astype(BF16)
        kxb = kx.astype(BF16)
        acc = jnp.zeros((tm, MIX), F32)
        for j in range(c):
            diff = jnp.where(pos >= j, g2 - row_of_chunk(g2, j), MASK_VALUE)
            w = _group_sum(qb * row_of_chunk(kxb, j) * jnp.exp2(diff).astype(BF16), e)
            yield
            acc = acc + w * row_of_chunk(v, j)
        band.append(acc)

    def inter():
        states = [s_ref[p] for p in range(N_PAIRS)]
        for ci in range(n):
            rows = slice(ci * c, (ci + 1) * c)
            for p in range(N_PAIRS):
                sl = slice(p * LANES, (p + 1) * LANES)
                upd = _dot_tn(vb_ref[rows, sl], kd_ref[rows, sl])
                os_ref[rows, sl] = _dot_nt(qd_ref[rows, sl], states[p].astype(BF16))
                dec = jnp.exp(ge_ref[ci * c:ci * c + 1, sl])
                states[p] = states[p] * dec + jnp.where(same_head, upd, 0.0)
            yield
        for p in range(N_PAIRS):
            s_ref[p] = states[p]

    _run_lockstep([intra(), inter()])
    acc = band[0]

    o = acc + os_ref[...]
    ms = _group_sum(o * o, e) * (1.0 / HEAD)
    og = og_ref[...].astype(F32)
    o_ref[...] = (o * lax.rsqrt(ms + NORM_EPS) * gain_ref[...] * (og * _sigmoid(og))).astype(o_ref.dtype)


def _hgrn(main, aux, lower, o_gain, bsz, seq, tm):
    nt = seq // tm
    ch = COL_HGRN // MIX
    row = lambda cidx: (lambda b, i: (b * nt + i, cidx))
    const = lambda b, i: (0, 0)
    return pl.pallas_call(
        _hgrn_kernel,
        grid=(bsz, nt),
        in_specs=[pl.BlockSpec((tm, MIX), row(ch)),
                  pl.BlockSpec((tm, MIX), row(ch + 1)),
                  pl.BlockSpec((tm, MIX), row(ch + 2)),
                  pl.BlockSpec((tm, MIX), row(0)),
                  pl.BlockSpec((1, MIX), const),
                  pl.BlockSpec((1, MIX), const),
                  pl.BlockSpec((MIX // 2, MIX // 2), const),
                  pl.BlockSpec((tm, tm), const)],
        out_specs=pl.BlockSpec((tm, MIX), lambda b, i: (b * nt + i, 0)),
        out_shape=jax.ShapeDtypeStruct((bsz * seq, MIX), BF16),
        scratch_shapes=[pltpu.VMEM((N_PAIRS, LANES, LANES), F32),
                        pltpu.VMEM((tm, MIX), BF16),
                        pltpu.VMEM((tm, MIX), BF16),
                        pltpu.VMEM((tm, MIX), BF16),
                        pltpu.VMEM((tm, MIX), F32),
                        pltpu.VMEM((tm, MIX), F32)],
        compiler_params=_cparams(("parallel", "arbitrary")),
        name="hgrn2",
    )(main, main, main, aux, lower, o_gain, _block_ones(MIX // 2), _block_tril(tm, HGRN_SUB))


def _rwkv_prep_kernel(*refs, has_vres):
    if has_vres:
        (p_ref, ph_ref, mu_ref, w0_ref, a0_ref, wa2_ref, g2_ref, kkk_ref, ka_ref, e_ref,
         misc_ref, misch_ref, vmu_ref, v0_ref, v2_ref, vf_ref,
         r_o, k_o, v_o, kk_o, a_o, lw_o, g_o) = refs
    else:
        (p_ref, ph_ref, mu_ref, w0_ref, a0_ref, wa2_ref, g2_ref, kkk_ref, ka_ref, e_ref,
         r_o, k_o, v_o, kk_o, a_o, lw_o, g_o) = refs
    tm = p_ref.shape[0]
    first = pl.program_id(1) == 0

    def token_shift(cur, halo_ref):
        prev_last = halo_ref[halo_ref.shape[0] - 1:, :].astype(F32)
        prev_last = jnp.where(first, 0.0, prev_last)
        rolled = pltpu.roll(cur, 1, 0)
        row = lax.broadcasted_iota(jnp.int32, cur.shape, 0)
        return jnp.where(row == 0, prev_last, rolled)

    p = p_ref[...].astype(F32)
    pm = p + (token_shift(p, ph_ref) - p) * mu_ref[...]
    r = pm[:, 0:MIX]
    k_raw = pm[:, MIX:2 * MIX]
    v = pm[:, 2 * MIX:3 * MIX]
    lora = pm[:, 3 * MIX:3 * MIX + LANES]
    g_lo = pm[:, 3 * MIX + LANES:3 * MIX + 2 * LANES]
    lane = lax.broadcasted_iota(jnp.int32, (tm, LANES), 1)
    z = jnp.where(lane < HEAD, jnp.tanh(lora), lora)
    wa = jnp.dot(z.astype(BF16), wa2_ref[...], preferred_element_type=F32)
    w_raw = _log_sigmoid(w0_ref[...] + wa[:, :MIX]) - 0.5
    lw_o[...] = -jnp.exp(w_raw)
    a = _sigmoid(a0_ref[...] + wa[:, MIX:])
    g_o[...] = jnp.dot(_sigmoid(g_lo).astype(BF16), g2_ref[...], preferred_element_type=F32).astype(BF16)
    if has_vres:
        m = misc_ref[...]
        ms = m + (token_shift(m, misch_ref) - m) * vmu_ref[...]
        mix = _sigmoid(v0_ref[...] + jnp.dot(ms.astype(BF16), v2_ref[...], preferred_element_type=F32))
        v = v + (vf_ref[...].astype(F32) - v) * mix
    kk = k_raw * kkk_ref[...]
    ss = _group_sum(kk * kk, e_ref[...])
    kk_o[...] = (kk * lax.rsqrt(jnp.maximum(ss, 1e-24))).astype(BF16)
    r_o[...] = r.astype(BF16)
    k_o[...] = (k_raw * (1.0 + (a - 1.0) * ka_ref[...])).astype(BF16)
    v_o[...] = v.astype(BF16)
    a_o[...] = a.astype(BF16)


def _rwkv_prep(main, aux, prm, v_first, bsz, seq, tm):
    nt = seq // tm
    has_vres = v_first is not None
    pw = 4 * MIX
    cp = COL_RWKV // pw
    halo = 16
    row = lambda cidx: (lambda b, i: (b * nt + i, cidx))
    const = lambda b, i: (0, 0)

    def halo_map(rows, cidx):
        per = tm // rows
        return lambda b, i: (jnp.maximum((b * nt + i) * per - 1, 0), cidx)

    in_specs = [pl.BlockSpec((tm, pw), row(cp)),
                pl.BlockSpec((halo, pw), halo_map(halo, cp)),
                pl.BlockSpec((1, pw), const),
                pl.BlockSpec((1, MIX), const),
                pl.BlockSpec((1, MIX), const),
                pl.BlockSpec((LANES, 2 * MIX), const),
                pl.BlockSpec((LANES, MIX), const),
                pl.BlockSpec((1, MIX), const),
                pl.BlockSpec((1, MIX), const),
                pl.BlockSpec((MIX // 2, MIX // 2), const)]
    args = [main, main, prm["mu"], prm["w0"], prm["a0"], prm["wa2"], prm["g2"], prm["k_k"], prm["k_a"],
            _block_ones(MIX // 2)]
    if has_vres:
        in_specs += [pl.BlockSpec((tm, LANES), row(MIX // LANES)),
                     pl.BlockSpec((8, LANES), halo_map(8, MIX // LANES)),
                     pl.BlockSpec((1, LANES), const),
                     pl.BlockSpec((1, MIX), const),
                     pl.BlockSpec((LANES, MIX), const),
                     pl.BlockSpec((tm, MIX), row(0))]
        args += [aux, aux, prm["vmu"], prm["v0"], prm["v2"], v_first]
    ospec = pl.BlockSpec((tm, MIX), lambda b, i: (b * nt + i, 0))
    o16 = jax.ShapeDtypeStruct((bsz * seq, MIX), BF16)
    o32 = jax.ShapeDtypeStruct((bsz * seq, MIX), F32)
    return pl.pallas_call(
        functools.partial(_rwkv_prep_kernel, has_vres=has_vres),
        grid=(bsz, nt),
        in_specs=in_specs,
        out_specs=[ospec] * 7,
        out_shape=[o16, o16, o16, o16, o16, o32, o16],
        compiler_params=_cparams(("parallel", "arbitrary")),
        name="rwkv_prep",
    )(*args)


def _rwkv_scan_kernel(r_ref, k_ref, v_ref, kk_ref, a_ref, lw_ref, g_ref, rk_ref, lnw_ref, lnb_ref,
                      tri_ref, e_ref, o_ref,
                      s_ref, kq_s, rq_s, bk_s, kx_s, nb_s, kd_s, dec_s, tkq_s, rqe_s, w_s, ol_s, oo_s):
    tm = r_ref.shape[0]
    c = RWKV_CHUNK
    n = tm // c

    @pl.when(pl.program_id(1) == 0)
    def _():
        s_ref[...] = jnp.zeros_like(s_ref)

    group = 2
    gr = group * c
    n_groups = n // group

    def prep(gi):
        rows = slice(gi * gr, (gi + 1) * gr)
        lw = lw_ref[rows, :]
        gam = _exact_dot(tri_ref[0:gr, 0:gr], lw)
        yield
        g_last = jnp.broadcast_to(gam.reshape(group, c, MIX)[:, c - 1:c, :], (group, c, MIX)).reshape(gr, MIX)
        kk = kk_ref[rows, :].astype(F32)
        kq_s[rows, :] = (kk * jnp.exp(gam - lw)).astype(BF16)
        rq_s[rows, :] = (r_ref[rows, :].astype(F32) * jnp.exp(gam)).astype(BF16)
        yield
        b = a_ref[rows, :].astype(F32) * kk
        k = k_ref[rows, :].astype(F32)
        e_inv = jnp.exp(-gam)
        bk_s[rows, :] = (b * e_inv).astype(BF16)
        kx_s[rows, :] = (k * e_inv).astype(BF16)
        yield
        e_dec = jnp.exp(g_last - gam)
        nb_s[rows, :] = (-(b * e_dec)).astype(BF16)
        kd_s[rows, :] = (k * e_dec).astype(BF16)
        dec_s[rows, :] = jnp.exp(g_last)
        yield

    lane = lax.broadcasted_iota(jnp.int32, (c, LANES), 1)
    t_idx = lax.broadcasted_iota(jnp.int32, (c, LANES), 0)
    s_idx = lane % HEAD
    left = lane < HEAD
    strict = t_idx > s_idx
    incl = t_idx >= s_idx
    eye = jnp.where(t_idx == s_idx, 1.0, 0.0)
    lane_r = lax.broadcasted_iota(jnp.int32, (LANES, LANES), 0) // HEAD
    lane_c = lax.broadcasted_iota(jnp.int32, (LANES, LANES), 1) // HEAD
    same_head = lane_r == lane_c

    def halves(x):
        zero = jnp.zeros_like(x)
        return jnp.where(left, x, zero), jnp.where(left, zero, x)

    def bdiag(x):
        x0, x1 = halves(x.astype(BF16))
        return jnp.concatenate([x0, x1], axis=0)

    def mm(a, rhs):
        return jnp.dot(a.astype(BF16), rhs, preferred_element_type=F32)


    def local_chain(start, p):
        rows = slice(start, start + c)
        sl = slice(p * LANES, (p + 1) * LANES)
        kq, rq = kq_s[rows, sl], rq_s[rows, sl]
        lhs = jnp.concatenate([kq, rq], axis=0)
        rhs = jnp.concatenate(halves(bk_s[rows, sl]) + halves(kx_s[rows, sl]), axis=0)
        aa = _dot_nt(lhs, rhs)
        yield
        a_ab = jnp.where(strict, aa[:c, :LANES], 0.0)
        a_ak = jnp.where(strict, aa[:c, LANES:], 0.0)
        a_rb = jnp.where(incl, aa[c:, :LANES], 0.0)
        a_rk = jnp.where(incl, aa[c:, LANES:], 0.0)
        yo = mm(jnp.concatenate([a_ak, a_rk], axis=0), bdiag(v_ref[rows, sl]))
        yield
        pw = -a_ab
        tinv = eye + pw
        pw = mm(pw, bdiag(pw))
        yield
        for _ in range(4):
            res = mm(pw, jnp.concatenate([bdiag(tinv), bdiag(pw)], axis=1))
            yield
            tinv = tinv + res[:, :LANES]
            pw = res[:, LANES:]
        tinv = tinv + mm(pw, bdiag(tinv))
        yield
        tw = mm(tinv, jnp.concatenate([bdiag(kq), bdiag(yo[:c])], axis=1))
        yield
        ar = mm(a_rb, jnp.concatenate([bdiag(tw[:, :LANES]), bdiag(tw[:, LANES:])], axis=1))
        yield
        tkq_s[rows, sl] = tw[:, :LANES].astype(BF16)
        w_s[rows, sl] = tw[:, LANES:]
        rqe_s[rows, sl] = (rq.astype(F32) - ar[:, :LANES]).astype(BF16)
        ol_s[rows, sl] = yo[c:] - ar[:, LANES:]

    def seq_group(gi):
        for u in range(group):
            start = (gi * group + u) * c
            rows = slice(start, start + c)
            states = [s_ref[p] for p in range(N_PAIRS)]
            xs = []
            for p in range(N_PAIRS):
                sl = slice(p * LANES, (p + 1) * LANES)
                lhs = jnp.concatenate([tkq_s[rows, sl], rqe_s[rows, sl]], axis=0)
                xs.append(_dot_nt(lhs, states[p].astype(BF16)))
            yield
            upds = []
            for p in range(N_PAIRS):
                sl = slice(p * LANES, (p + 1) * LANES)
                uu = xs[p][:c] + w_s[rows, sl]
                oo_s[rows, sl] = xs[p][c:] + ol_s[rows, sl]
                wv = jnp.concatenate([uu.astype(BF16), v_ref[rows, sl]], axis=0)
                wk = jnp.concatenate([nb_s[rows, sl], kd_s[rows, sl]], axis=0)
                upds.append(_dot_tn(wv, wk))
            yield
            for p in range(N_PAIRS):
                sl = slice(p * LANES, (p + 1) * LANES)
                s_ref[p] = states[p] * dec_s[start:start + 1, sl] + jnp.where(same_head, upds[p], 0.0)

    def local_group(gi):
        return [local_chain((gi * group + u) * c, p) for u in range(group) for p in range(N_PAIRS)]

    def post(gi):
        rows = slice(gi * gr, (gi + 1) * gr)
        e = e_ref[...]
        o = oo_s[rows, :]
        mean = _group_sum2(o, e) * (1.0 / HEAD)
        yield
        d = o - mean
        var = _group_sum2(d * d, e) * (1.0 / HEAD)
        yield
        v = v_ref[rows, :].astype(F32)
        bonus = _group_sum2(r_ref[rows, :].astype(F32) * k_ref[rows, :].astype(F32) * rk_ref[...], e)
        yield
        out = d * lax.rsqrt(var + RWKV_LN_EPS) * lnw_ref[...] + lnb_ref[...] + bonus * v
        o_ref[rows, :] = (out * g_ref[rows, :].astype(F32)).astype(o_ref.dtype)

    for step in range(n_groups + 3):
        chains = []
        if step < n_groups:
            chains.append(prep(step))
        if 0 <= step - 1 < n_groups:
            chains += local_group(step - 1)
        if 0 <= step - 2 < n_groups:
            chains.append(seq_group(step - 2))
        if 0 <= step - 3 < n_groups:
            chains.append(post(step - 3))
        _run_lockstep(chains)


def _rwkv_scan(r, k, v, kk, a, lw, g, r_k, ln_w, ln_b, bsz, seq, tm):
    nt = seq // tm
    spec = pl.BlockSpec((tm, MIX), lambda b, i: (b * nt + i, 0))
    const = lambda b, i: (0, 0)
    vec = pl.BlockSpec((1, MIX), const)
    s16 = pltpu.VMEM((tm, MIX), BF16)
    s32 = pltpu.VMEM((tm, MIX), F32)
    return pl.pallas_call(
        _rwkv_scan_kernel,
        grid=(bsz, nt),
        in_specs=[spec] * 7 + [vec, vec, vec,
                               pl.BlockSpec((tm, tm), const),
                               pl.BlockSpec((MIX // 2, MIX // 2), const)],
        out_specs=spec,
        out_shape=jax.ShapeDtypeStruct((bsz * seq, MIX), BF16),
        scratch_shapes=[pltpu.VMEM((N_PAIRS, LANES, LANES), F32),
                        s16, s16, s16, s16, s16, s16, s32, s16, s16, s32, s32, s32],
        compiler_params=_cparams(("parallel", "arbitrary")),
        name="rwkv_scan",
    )(r, k, v, kk, a, lw, g, r_k, ln_w, ln_b, _block_tril(tm, RWKV_CHUNK), _block_ones(MIX // 2))


def _merge_kernel(yf_ref, yh_ref, yr_ref, gf_ref, gh_ref, gr_ref, x_ref, gt_ref, wb_ref, wo_ref, o_ref):
    def branch(y_ref, g_ref, i):
        return _sigmoid(g_ref[...].astype(F32)) * jnp.dot(y_ref[...], wb_ref[i], preferred_element_type=F32)

    merged = branch(yf_ref, gf_ref, 0) + branch(yh_ref, gh_ref, 1) + branch(yr_ref, gr_ref, 2)
    o_ref[...] = x_ref[...] + gt_ref[0] * jnp.dot(merged.astype(BF16), wo_ref[...], preferred_element_type=F32)


def _merge(y_fox, y_hgrn, y_rwkv, main, x, gt, w_branch, w_out, seq, tm):
    t, d = x.shape
    tps = seq // tm
    yspec = pl.BlockSpec((tm, MIX), lambda i: (i, 0))
    gspec = lambda cidx: pl.BlockSpec((tm, d), lambda i: (i, cidx))
    return pl.pallas_call(
        _merge_kernel,
        grid=(t // tm,),
        in_specs=[yspec, yspec, yspec, gspec(0), gspec(1), gspec(2),
                  pl.BlockSpec((tm, d), lambda i: (i, 0)),
                  pl.BlockSpec((1, 1, d), lambda i: (i // tps, 0, 0)),
                  pl.BlockSpec((3, MIX, d), lambda i: (0, 0, 0)),
                  pl.BlockSpec((d, d), lambda i: (0, 0))],
        out_specs=pl.BlockSpec((tm, d), lambda i: (i, 0)),
        out_shape=jax.ShapeDtypeStruct((t, d), F32),
        compiler_params=_cparams(("parallel",)),
        name="merge_out",
    )(y_fox, y_hgrn, y_rwkv, main, main, main, x, gt, w_branch, w_out)


FFN_HALO = 16


FFN_TF = 256


def _ffn_body(x_ref, xh_ref, sc_ref, sh_ref, gt_ref, wup_ref, cw_ref, cb_ref, wd_ref, o_ref, h_ref, act_ref,
              *, tps):
    def normed(x):
        ms = jnp.mean(x * x, axis=-1, keepdims=True)
        return x * lax.rsqrt(ms + NORM_EPS) * sc_ref[0] + sh_ref[0]

    seq_start = pl.program_id(0) % tps == 0
    h_ref[0:FFN_HALO, :] = jnp.where(seq_start, 0.0, normed(xh_ref[...])).astype(BF16)
    h_ref[FFN_HALO:, :] = normed(x_ref[...]).astype(BF16)

    def up_proj(col):
        return jnp.dot(h_ref[...], wup_ref[:, col:col + FFN_TF], preferred_element_type=F32)

    def conv(u, col):
        cw = cw_ref[:, col:col + FFN_TF]
        y = cb_ref[:, col:col + FFN_TF] + cw[0:1] * pltpu.roll(u, 2, 0) + cw[1:2] * pltpu.roll(u, 1, 0) + cw[2:3] * u
        return y[FFN_HALO:]

    cols = [base + j * FFN_TF for j in range(D_FF // FFN_TF) for base in (0, D_FF)]
    u_next = up_proj(cols[0])
    val = None
    for i, col in enumerate(cols):
        u = u_next
        if i + 1 < len(cols):
            u_next = up_proj(cols[i + 1])
        y = conv(u, col)
        if i % 2 == 0:
            val = y
        else:
            act_ref[:, col - D_FF:col - D_FF + FFN_TF] = (y * _sigmoid(y) * val).astype(BF16)

    down = jnp.dot(act_ref[...], wd_ref[...], preferred_element_type=F32)
    o_ref[...] = x_ref[...] + gt_ref[0] * down


def _ffn(x, scale, shift, gt, w_up, conv_w, conv_b, w_down, seq, tm):
    t, d = x.shape
    tps = seq // tm
    per = tm // FFN_HALO
    bmap = lambda i: (i // tps, 0, 0)
    const = lambda i: (0, 0)
    resident = dict(pipeline_mode=pl.Buffered(1))
    return pl.pallas_call(
        functools.partial(_ffn_body, tps=tps),
        grid=(t // tm,),
        in_specs=[pl.BlockSpec((tm, d), lambda i: (i, 0)),
                  pl.BlockSpec((FFN_HALO, d), lambda i: (jnp.maximum(i * per - 1, 0), 0)),
                  pl.BlockSpec((1, 1, d), bmap),
                  pl.BlockSpec((1, 1, d), bmap),
                  pl.BlockSpec((1, 1, d), bmap),
                  pl.BlockSpec((d, 2 * D_FF), const, **resident),
                  pl.BlockSpec((3, 2 * D_FF), const),
                  pl.BlockSpec((1, 2 * D_FF), const),
                  pl.BlockSpec((D_FF, d), const, **resident)],
        out_specs=pl.BlockSpec((tm, d), lambda i: (i, 0)),
        out_shape=jax.ShapeDtypeStruct((t, d), F32),
        scratch_shapes=[pltpu.VMEM((tm + FFN_HALO, d), BF16), pltpu.VMEM((tm, D_FF), BF16)],
        compiler_params=_cparams(("parallel",)),
        name="conv_mlp",
    )(x, x, scale, shift, gt, w_up, conv_w, conv_b, w_down)


def _pack_in_proj(w_in, vres_down):
    depth, d, _ = w_in.shape
    fox, hgrn, rwkv, gate = 0, 1544, 3592, 5384
    z = lambda n: jnp.zeros((depth, d, n), w_in.dtype)
    main = jnp.concatenate([
        w_in[..., gate:gate + 3072],
        w_in[..., fox:fox + 1536],
        w_in[..., hgrn:hgrn + 512], w_in[..., hgrn + 1024:hgrn + 2048],
        w_in[..., rwkv:rwkv + 1792],
        z(N_MAIN - 7936)], axis=-1).astype(BF16)
    vres = jnp.concatenate([jnp.zeros((1, d, VRES_RANK), w_in.dtype), vres_down], axis=0)
    aux = jnp.concatenate([
        w_in[..., hgrn + 512:hgrn + 1024],
        w_in[..., fox + 1536:fox + 1544], z(AUX_VRES_LANE - 8),
        vres, z(LANES - AUX_VRES_LANE - VRES_RANK)], axis=-1).astype(BF16)
    return main, aux


def _tile_heads(v):
    return jnp.tile(v, N_HEADS).reshape(1, MIX)


def _forward(x, c, w_ada, b_ada, norm1_g, norm2_g, w_in, fox_b_f, fox_q_gain, fox_k_gain,
             hgrn_lb, hgrn_o_gain, rwkv_mu, rwkv_w0, rwkv_w2, rwkv_a0, rwkv_a2, rwkv_g2,
             rwkv_k_k, rwkv_k_a, rwkv_r_k, rwkv_ln_w, rwkv_ln_b, rwkv_vres_down, rwkv_vres_mu,
             rwkv_v0, rwkv_v2, w_branch, w_out, w_up, conv_w, conv_b, w_down, *, tiles):
    bsz, seq, d = x.shape
    depth = w_in.shape[0]
    t = bsz * seq

    lb_prob = jax.nn.softmax(hgrn_lb.astype(F32), axis=0)
    hgrn_lower = jnp.cumsum(lb_prob, axis=0) - lb_prob[0]

    mod = _ada_mod(c, w_ada, b_ada)
    w_main, w_aux = _pack_in_proj(w_in, rwkv_vres_down)
    w_branch16, w_out16 = w_branch.astype(BF16), w_out.astype(BF16)
    w_up16, w_down16 = w_up.astype(BF16), w_down.astype(BF16)

    xf = x.reshape(t, d)
    v_first = None
    for l in range(depth):
        sh1, sc1, gt1, sh2, sc2, gt2 = [m.reshape(bsz, 1, d) for m in jnp.split(mod[l], 6, axis=-1)]
        scale1 = norm1_g[l] * (1.0 + sc1)
        main, aux = _in_proj(xf, scale1, sh1, w_main[l], w_aux[l], seq, tiles["proj_tm"], tiles["proj_tn"])

        b_f = jnp.zeros((1, LANES), F32).at[0, AUX_FF_LANE:AUX_FF_LANE + N_HEADS].set(fox_b_f[l])
        qh, kh, vh = _fox_prep(main, aux, _tile_heads(fox_q_gain[l]) * HEAD ** -0.5, _tile_heads(fox_k_gain[l]),
                               b_f, bsz, seq, tiles["fox_prep_tm"])
        y_fox = _fox_attn(qh, kh, vh, tiles["fox_tq"], tiles["fox_sub"], tiles["fox_tk"])

        y_hgrn = _hgrn(main, aux, hgrn_lower[l].reshape(1, MIX), _tile_heads(hgrn_o_gain[l]), bsz, seq,
                       tiles["hgrn_tm"])

        wa2 = jnp.zeros((LANES, 2 * MIX), F32)
        wa2 = wa2.at[:HEAD, :MIX].set(rwkv_w2[l]).at[HEAD:, MIX:].set(rwkv_a2[l])
        prm = {
            "mu": jnp.concatenate([rwkv_mu[l], jnp.zeros((4 * MIX - rwkv_mu.shape[1],), F32)]).reshape(1, 4 * MIX),
            "w0": rwkv_w0[l].reshape(1, MIX), "a0": rwkv_a0[l].reshape(1, MIX),
            "wa2": wa2.astype(BF16), "g2": rwkv_g2[l].astype(BF16),
            "k_k": rwkv_k_k[l].reshape(1, MIX), "k_a": rwkv_k_a[l].reshape(1, MIX),
        }
        if l > 0:
            vmu = jnp.zeros((1, LANES), F32).at[0, AUX_VRES_LANE:AUX_VRES_LANE + VRES_RANK].set(rwkv_vres_mu[l - 1])
            v2 = jnp.zeros((LANES, MIX), F32).at[AUX_VRES_LANE:AUX_VRES_LANE + VRES_RANK].set(rwkv_v2[l - 1])
            prm.update(vmu=vmu, v0=rwkv_v0[l - 1].reshape(1, MIX), v2=v2.astype(BF16))
        r, k, v, kk, a, lw, g = _rwkv_prep(main, aux, prm, v_first, bsz, seq, tiles["rwkv_prep_tm"])
        if l == 0:
            v_first = v
        y_rwkv = _rwkv_scan(r, k, v, kk, a, lw, g, rwkv_r_k[l].reshape(1, MIX), rwkv_ln_w[l].reshape(1, MIX),
                            rwkv_ln_b[l].reshape(1, MIX), bsz, seq, tiles["rwkv_tm"])

        xf = _merge(y_fox, y_hgrn, y_rwkv, main, xf, gt1, w_branch16[l], w_out16[l], seq, tiles["merge_tm"])

        scale2 = norm2_g[l] * (1.0 + sc2)
        xf = _ffn(xf, scale2, sh2, gt2, w_up16[l], conv_w[l], conv_b[l].reshape(1, -1), w_down16[l], seq,
                  tiles["ffn_tm"])
    return xf.reshape(bsz, seq, d)


def _tiles_for(seq):
    cap = lambda n: min(n, seq)
    return dict(proj_tm=cap(1024), proj_tn=1024, fox_prep_tm=cap(512), fox_tq=cap(1024), fox_sub=256,
                fox_tk=cap(1024), hgrn_tm=cap(256), rwkv_prep_tm=cap(512), rwkv_tm=cap(512), merge_tm=cap(1024),
                ffn_tm=cap(1024))


def kernel(x, c, w_ada, b_ada, norm1_g, norm2_g, w_in, fox_b_f, fox_q_gain, fox_k_gain, hgrn_lb, hgrn_o_gain, rwkv_mu, rwkv_w0, rwkv_w2, rwkv_a0, rwkv_a2, rwkv_g2, rwkv_k_k, rwkv_k_a, rwkv_r_k, rwkv_ln_w, rwkv_ln_b, rwkv_vres_down, rwkv_vres_mu, rwkv_v0, rwkv_v2, w_branch, w_out, w_up, conv_w, conv_b, w_down):
    return _forward(x, c, w_ada, b_ada, norm1_g, norm2_g, w_in, fox_b_f, fox_q_gain, fox_k_gain,
                    hgrn_lb, hgrn_o_gain, rwkv_mu, rwkv_w0, rwkv_w2, rwkv_a0, rwkv_a2, rwkv_g2,
                    rwkv_k_k, rwkv_k_a, rwkv_r_k, rwkv_ln_w, rwkv_ln_b, rwkv_vres_down, rwkv_vres_mu,
                    rwkv_v0, rwkv_v2, w_branch, w_out, w_up, conv_w, conv_b, w_down,
                    tiles=_tiles_for(x.shape[1]))
```

```python
import functools

import jax
import jax.numpy as jnp
import numpy as np
from jax import lax
from jax.experimental import pallas as pl
from jax.experimental.pallas import tpu as pltpu

F32 = jnp.float32
BF16 = jnp.bfloat16

MIX = 512
HEAD = 64
N_HEADS = MIX // HEAD
N_PAIRS = N_HEADS // 2
LANES = 128
D_FF = 2816
NORM_EPS = 1e-6
RWKV_LN_EPS = 64e-5
MASK_VALUE = -1e30
LOG_FLOOR = 1e-30
LOG2E = 1.4426950408889634
VRES_RANK = 32

HGRN_SUB = 16
RWKV_CHUNK = 64

COL_GATE = 0
COL_FOX = 3072
COL_HGRN = 4608
COL_RWKV = 6144
N_MAIN = 8192
N_AUX = 640
AUX_FF_LANE = 0
AUX_VRES_LANE = 32

VMEM_LIMIT = 56 * 1024 * 1024


def _cparams(sem):
    return pltpu.CompilerParams(dimension_semantics=sem, vmem_limit_bytes=VMEM_LIMIT)


def _split3(x):
    hi = x.astype(BF16)
    r1 = x - hi.astype(F32)
    mid = r1.astype(BF16)
    lo = (r1 - mid.astype(F32)).astype(BF16)
    return hi, mid, lo


def _exact_dot(a01, x):
    hi, mid, lo = _split3(x)
    d = lambda b: jnp.dot(a01, b, preferred_element_type=F32)
    return d(hi) + d(mid) + d(lo)


def _group_sum(x, e):
    w = e.shape[0]
    xb = x.astype(BF16)
    if x.shape[1] == w:
        return jnp.dot(xb, e, preferred_element_type=F32)
    parts = [jnp.dot(xb[:, i:i + w], e, preferred_element_type=F32) for i in range(0, x.shape[1], w)]
    return jnp.concatenate(parts, axis=1)


def _group_sum2(x, e):
    hi = x.astype(BF16)
    lo = x - hi.astype(F32)
    return _group_sum(hi, e) + _group_sum(lo, e)


def _sigmoid(x):
    return 1.0 / (1.0 + jnp.exp(-x))


def _log_sigmoid(x):
    return jnp.minimum(x, 0.0) - jnp.log1p(jnp.exp(-jnp.abs(x)))


def _dot_nt(a, b):
    return lax.dot_general(a, b, (((1,), (1,)), ((), ())), preferred_element_type=F32)


def _dot_tn(a, b):
    return lax.dot_general(a, b, (((0,), (0,)), ((), ())), preferred_element_type=F32)


def _run_lockstep(gens):
    live = list(gens)
    while live:
        nxt = []
        for g in live:
            try:
                next(g)
                nxt.append(g)
            except StopIteration:
                pass
        live = nxt


def _lockstep_rounds(gens):
    live = list(gens)
    while live:
        nxt = []
        for g in live:
            try:
                next(g)
                nxt.append(g)
            except StopIteration:
                pass
        live = nxt
        yield


def _block_ones(width):
    g = np.arange(width) // HEAD
    return jnp.asarray(g[:, None] == g[None, :], dtype=BF16)


def _block_tril(n, blk):
    i = np.arange(n)
    m = (i[:, None] >= i[None, :]) & ((i[:, None] // blk) == (i[None, :] // blk))
    return jnp.asarray(m, dtype=BF16)


def _ada_kernel(c_ref, w_ref, b_ref, o_ref):
    c = c_ref[...]
    cond = c * _sigmoid(c)
    o_ref[0] = jnp.dot(cond.astype(BF16), w_ref[0].astype(BF16), preferred_element_type=F32) + b_ref[0]


def _ada_mod(c, w_ada, b_ada, tn=1536):
    depth, d, n = w_ada.shape
    bsz = c.shape[0]
    return pl.pallas_call(
        _ada_kernel,
        grid=(depth, n // tn),
        in_specs=[pl.BlockSpec((bsz, d), lambda l, j: (0, 0)),
                  pl.BlockSpec((1, d, tn), lambda l, j: (l, 0, j)),
                  pl.BlockSpec((1, 1, tn), lambda l, j: (l, 0, j))],
        out_specs=pl.BlockSpec((1, bsz, tn), lambda l, j: (l, 0, j)),
        out_shape=jax.ShapeDtypeStruct((depth, bsz, n), F32),
        compiler_params=_cparams(("parallel", "parallel")),
        name="ada_mod",
    )(c, w_ada, b_ada.reshape(depth, 1, n))


FOX_PREP_ROWS = 512


def _fox_prep_stages(qkv_s, misc_s, qg_ref, kg_ref, bf_ref, e_ref, tri_ref, sel_ref,
                     qo_ref, ko_ref, vo_ref, carry_ref, row0):
    tm = FOX_PREP_ROWS
    blk = slice(row0, row0 + tm)
    e = e_ref[...]

    def qk_norm(cols, gain):
        xx = qkv_s[blk, cols].astype(F32)
        ss = _group_sum(xx * xx, e)
        return xx * lax.rsqrt(ss * (1.0 / HEAD) + NORM_EPS) * gain

    qn = qk_norm(slice(0, MIX), qg_ref[...])
    yield
    kn = qk_norm(slice(MIX, 2 * MIX), kg_ref[...])
    vv = qkv_s[blk, 2 * MIX:3 * MIX].astype(F32)
    yield

    lf = _log_sigmoid(misc_s[blk, :] + bf_ref[...])
    cum = _exact_dot(tri_ref[...], lf) + carry_ref[...]
    carry_ref[...] = cum[tm - 1:tm, :]
    yield
    hi, mid, lo = [s.astype(F32) for s in _split3(cum)]
    lane = lax.broadcasted_iota(jnp.int32, (tm, LANES), 1)
    fcat = jnp.where(lane < 8, hi, jnp.where(lane < 16, pltpu.roll(mid, 8, 1),
                     jnp.where(lane < 24, pltpu.roll(lo, 16, 1), jnp.where(lane == 24, 1.0, 0.0))))
    ext = jnp.dot(fcat.astype(BF16), sel_ref[...], preferred_element_type=F32)
    yield

    for p in range(N_PAIRS):
        sl = slice(p * LANES, (p + 1) * LANES)
        qp, kp, vp = qn[:, sl], kn[:, sl], vv[:, sl]
        for par in range(2):
            h = 2 * p + par
            base = HEAD if par == 0 else 0
            keep = (lane < HEAD) if par == 0 else (lane >= HEAD)
            qo_ref[0, h, :, blk] = jnp.where(keep, qp, ext[:, 2 * h * LANES:(2 * h + 1) * LANES]).T.astype(BF16)
            ko_ref[0, h, blk, :] = jnp.where(keep, kp, ext[:, (2 * h + 1) * LANES:(2 * h + 2) * LANES]).astype(BF16)
            vo_ref[0, h, :, blk] = jnp.where(keep, vp, jnp.where(lane == base, 1.0, 0.0)).T.astype(BF16)
        yield


def _fox_sel():
    sel = np.zeros((LANES, N_HEADS, 2, LANES), np.float32)
    for h in range(N_HEADS):
        base = HEAD if h % 2 == 0 else 0
        for i in range(3):
            sel[8 * i + h, h, 0, base + i] = 1.0
            sel[24, h, 0, base + 3 + i] = 1.0
            sel[24, h, 1, base + i] = 1.0
            sel[8 * i + h, h, 1, base + 3 + i] = -1.0
    return jnp.asarray(sel.reshape(LANES, N_HEADS * 2 * LANES), dtype=BF16)


def _in_proj_kernel(x_ref, sc_ref, sh_ref, w_ref, wa_ref, qg_ref, kg_ref, bf_ref, e_ref, tri_ref, sel_ref,
                    o_ref, oa_ref, qo_ref, ko_ref, vo_ref, h_ref, qkv_s, misc_s, carry_ref, *, n_main, tps):
    j = pl.program_id(1)
    tm, tn = o_ref.shape
    fox_blk = COL_FOX // tn + 1
    prep_rows = list(range(0, tm, FOX_PREP_ROWS))
    prep_steps = [fox_blk + 2 + u for u in range(len(prep_rows))]

    @pl.when(j == 0)
    def _():
        x = x_ref[...]
        ms = jnp.mean(x * x, axis=-1, keepdims=True)
        h_ref[...] = (x * lax.rsqrt(ms + NORM_EPS) * sc_ref[0] + sh_ref[0]).astype(BF16)
        aux = jnp.dot(h_ref[...], wa_ref[...], preferred_element_type=F32)
        oa_ref[...] = aux
        misc_s[...] = aux[:, MIX:MIX + LANES]

        @pl.when(pl.program_id(0) % tps == 0)
        def _():
            carry_ref[...] = jnp.zeros_like(carry_ref)

    def block():
        return jnp.dot(h_ref[...], w_ref[...], preferred_element_type=F32).astype(BF16)

    plain = (j >= 1) & (j != fox_blk) & (j != fox_blk + 1)
    for st in prep_steps:
        plain = plain & (j != st)

    @pl.when(plain)
    def _():
        o_ref[...] = block()

    @pl.when(j == fox_blk)
    def _():
        o = block()
        o_ref[...] = o
        qkv_s[:, 0:2 * MIX] = o[:, 0:2 * MIX]

    @pl.when(j == fox_blk + 1)
    def _():
        o = block()
        o_ref[...] = o
        qkv_s[:, 2 * MIX:3 * MIX] = o[:, 0:MIX]

    for st, row0 in zip(prep_steps, prep_rows):
        @pl.when(j == st)
        def _(row0=row0):
            def matmul():
                sub = tm // 4
                for r in range(0, tm, sub):
                    o_ref[r:r + sub, :] = jnp.dot(h_ref[r:r + sub, :], w_ref[...],
                                                  preferred_element_type=F32).astype(BF16)
                    yield

            _run_lockstep([matmul(), _fox_prep_stages(qkv_s, misc_s, qg_ref, kg_ref, bf_ref, e_ref, tri_ref,
                                                       sel_ref, qo_ref, ko_ref, vo_ref, carry_ref, row0)])


def _in_proj(x, scale, shift, w, w_aux, q_gain, k_gain, b_f, bsz, seq, tm, tn):
    t, d = x.shape
    n = w.shape[1]
    n_aux = w_aux.shape[1]
    n_main = n // tn
    tps = seq // tm
    assert COL_FOX % tn == 0 and 2 * MIX == tn and tm % FOX_PREP_ROWS == 0
    assert COL_FOX // tn + 3 + tm // FOX_PREP_ROWS <= n_main + 1
    col = lambda i, j: (0, jnp.maximum(j - 1, 0))
    const = lambda i, j: (0, 0)
    head = lambda i, j: (i // tps, 0, i % tps, 0)
    head_t = lambda i, j: (i // tps, 0, 0, i % tps)
    out = jax.ShapeDtypeStruct((bsz, N_HEADS, seq, LANES), BF16)
    out_t = jax.ShapeDtypeStruct((bsz, N_HEADS, LANES, seq), BF16)
    return pl.pallas_call(
        functools.partial(_in_proj_kernel, n_main=n_main, tps=tps),
        grid=(t // tm, n_main + 1),
        in_specs=[pl.BlockSpec((tm, d), lambda i, j: (i, 0)),
                  pl.BlockSpec((1, 1, d), lambda i, j: (i // tps, 0, 0)),
                  pl.BlockSpec((1, 1, d), lambda i, j: (i // tps, 0, 0)),
                  pl.BlockSpec((d, tn), col),
                  pl.BlockSpec((d, n_aux), const),
                  pl.BlockSpec((1, MIX), const),
                  pl.BlockSpec((1, MIX), const),
                  pl.BlockSpec((1, LANES), const),
                  pl.BlockSpec((MIX // 2, MIX // 2), const),
                  pl.BlockSpec((FOX_PREP_ROWS, FOX_PREP_ROWS), const),
                  pl.BlockSpec((LANES, N_HEADS * 2 * LANES), const)],
        out_specs=[pl.BlockSpec((tm, tn), lambda i, j: (i, jnp.maximum(j - 1, 0))),
                   pl.BlockSpec((tm, n_aux), lambda i, j: (i, 0)),
                   pl.BlockSpec((1, N_HEADS, LANES, tm), head_t),
                   pl.BlockSpec((1, N_HEADS, tm, LANES), head),
                   pl.BlockSpec((1, N_HEADS, LANES, tm), head_t)],
        out_shape=[jax.ShapeDtypeStruct((t, n), BF16), jax.ShapeDtypeStruct((t, n_aux), F32), out_t, out, out_t],
        scratch_shapes=[pltpu.VMEM((tm, d), BF16), pltpu.VMEM((tm, 3 * MIX), BF16), pltpu.VMEM((tm, LANES), F32),
                        pltpu.VMEM((1, LANES), F32)],
        compiler_params=_cparams(("arbitrary", "arbitrary")),
        name="in_proj",
    )(x, scale, shift, w, w_aux, q_gain, k_gain, b_f, _block_ones(MIX // 2),
      _block_tril(FOX_PREP_ROWS, FOX_PREP_ROWS), _fox_sel())


def _fox_attn_kernel(qt_ref, k_ref, vt_ref, o_ref, *, tq, sub, tk):
    qi = pl.program_id(2)
    n_sub = tq // sub
    kpt = tq // tk
    chains = [(hh, r) for hh in range(2) for r in range(n_sub)]

    def step(j, carry, key_off):
        new = list(carry)
        start = pl.multiple_of(j * tk, tk)

        def chain(idx, hh, r):
            q_lo = r * sub
            nk = tk if key_off is None else min(q_lo + sub - key_off, tk)
            if nk <= 0:
                return
            m, acc = carry[idx]
            qt = qt_ref[0, hh, :, q_lo:q_lo + sub]
            kb = k_ref[0, hh, pl.ds(start, nk), :]
            vtb = vt_ref[0, hh, :, pl.ds(start, nk)]
            st = jnp.dot(kb, qt, preferred_element_type=F32)
            yield
            if key_off is not None and key_off + nk > q_lo:
                kpos = lax.broadcasted_iota(jnp.int32, (nk, sub), 0) + key_off
                qpos = lax.broadcasted_iota(jnp.int32, (nk, sub), 1) + q_lo
                st = jnp.where(kpos <= qpos, st, MASK_VALUE)
            m_new = jnp.maximum(m, jnp.max(st, axis=0, keepdims=True))
            p = jnp.exp(st - m_new).astype(BF16)
            yield
            pv = jnp.dot(vtb, p, preferred_element_type=F32)
            yield
            new[idx] = (m_new, jnp.exp(m - m_new) * acc + pv)

        _run_lockstep([chain(i, hh, r) for i, (hh, r) in enumerate(chains)])
        return tuple(new)

    init = tuple((jnp.full((1, sub), MASK_VALUE, F32), jnp.zeros((LANES, sub), F32)) for _ in chains)
    carry = lax.fori_loop(0, qi * kpt, lambda j, c: step(j, c, None), init)
    for jd in range(kpt):
        carry = step(qi * kpt + jd, carry, jd * tk)
    row = lax.broadcasted_iota(jnp.int32, (LANES, tq), 0)
    accs = [jnp.concatenate([carry[hh * n_sub + r][1] for r in range(n_sub)], axis=1) for hh in range(2)]
    ot = jnp.where(row < HEAD, accs[0] / accs[0][HEAD:HEAD + 1, :], accs[1] / accs[1][0:1, :])
    o_ref[...] = ot.T.astype(o_ref.dtype)


def _fox_attn(qt, kh, vt, tq, sub, tk):
    bsz, _, seq, _ = kh.shape
    nq = seq // tq
    return pl.pallas_call(
        functools.partial(_fox_attn_kernel, tq=tq, sub=sub, tk=tk),
        grid=(bsz, N_PAIRS, nq),
        in_specs=[pl.BlockSpec((1, 2, LANES, tq), lambda b, p, i: (b, p, 0, i)),
                  pl.BlockSpec((1, 2, seq, LANES), lambda b, p, i: (b, p, 0, 0)),
                  pl.BlockSpec((1, 2, LANES, seq), lambda b, p, i: (b, p, 0, 0))],
        out_specs=pl.BlockSpec((tq, LANES), lambda b, p, i: (b * nq + i, p)),
        out_shape=jax.ShapeDtypeStruct((bsz * seq, MIX), BF16),
        compiler_params=_cparams(("parallel", "parallel", "arbitrary")),
        name="fox_attn",
    )(qt, kh, vt)


def _hgrn_stages(q_ref, i_ref, og_ref, f_ref, lb_ref, gain_ref, e_ref, tri_ref, o_ref,
                 s_ref, qd_ref, kd_ref, vb_ref, ge_ref, os_ref, row0):
    tm = qd_ref.shape[0]
    c = HGRN_SUB
    n = tm // c
    blk = slice(row0, row0 + tm)

    lb = lb_ref[...]
    gate = lb + (1.0 - lb) * _sigmoid(f_ref[blk, :])
    ell = jnp.log(jnp.maximum(gate, LOG_FLOOR))
    kx = 1.0 - gate
    qraw = q_ref[blk, :].astype(F32)
    q = qraw * _sigmoid(qraw)
    v = i_ref[blk, :].astype(F32)
    g = _exact_dot(tri_ref[...], ell)

    def row_of_chunk(a, j):
        a3 = a.reshape(n, c, MIX)
        return jnp.broadcast_to(a3[:, j:j + 1, :], (n, c, MIX)).reshape(tm, MIX)

    g_end = row_of_chunk(g, c - 1)
    qd_ref[...] = (q * jnp.exp(g)).astype(BF16)
    kd_ref[...] = (kx * jnp.exp(g_end - g)).astype(BF16)
    vb_ref[...] = v.astype(BF16)
    ge_ref[...] = g_end

    e = e_ref[...]
    pos = lax.broadcasted_iota(jnp.int32, (tm, MIX), 0) % c
    lane_r = lax.broadcasted_iota(jnp.int32, (LANES, LANES), 0) // HEAD
    lane_c = lax.broadcasted_iota(jnp.int32, (LANES, LANES), 1) // HEAD
    same_head = lane_r == lane_c
    band = []

    def intra():
        g2 = g * LOG2E
        qb = q.astype(BF16)
        kxb = kx.astype(BF16)
        acc = jnp.zeros((tm, MIX), F32)
        for j in range(c):
            diff = jnp.where(pos >= j, g2 - row_of_chunk(g2, j), MASK_VALUE)
            w = _group_sum(qb * row_of_chunk(kxb, j) * jnp.exp2(diff).astype(BF16), e)
            yield
            acc = acc + w * row_of_chunk(v, j)
        band.append(acc)

    def inter():
        states = [s_ref[p] for p in range(N_PAIRS)]
        for ci in range(n):
            rows = slice(ci * c, (ci + 1) * c)
            for p in range(N_PAIRS):
                sl = slice(p * LANES, (p + 1) * LANES)
                upd = _dot_tn(vb_ref[rows, sl], kd_ref[rows, sl])
                os_ref[rows, sl] = _dot_nt(qd_ref[rows, sl], states[p].astype(BF16))
                dec = jnp.exp(ge_ref[ci * c:ci * c + 1, sl])
                states[p] = states[p] * dec + jnp.where(same_head, upd, 0.0)
            yield
        for p in range(N_PAIRS):
            s_ref[p] = states[p]

    yield
    yield from _lockstep_rounds([intra(), inter()])
    acc = band[0]

    o = acc + os_ref[...]
    ms = _group_sum(o * o, e) * (1.0 / HEAD)
    og = og_ref[blk, :].astype(F32)
    o_ref[blk, :] = (o * lax.rsqrt(ms + NORM_EPS) * gain_ref[...] * (og * _sigmoid(og))).astype(o_ref.dtype)
    yield


def _rwkv_stages(src, rk_ref, lnw_ref, lnb_ref, tri_ref, e_ref, o_ref, vfo_ref,
                 r_ref, k_ref, v_ref, kk_ref, a_ref, lw_ref, g_ref,
                 s_ref, kq_s, rq_s, bk_s, kx_s, nb_s, kd_s, dec_s, tkq_s, rqe_s, w_s, ol_s, oo_s):
    tm = r_ref.shape[0]
    c = RWKV_CHUNK
    n = tm // c
    first = pl.program_id(1) == 0
    has_vres = "vf" in src

    group = 2
    gr = group * c
    n_groups = n // group

    def proj(gi):
        g0 = gi * gr
        rows = slice(g0, g0 + gr)

        def token_shift(cur, ref, halo_ref):
            if gi == 0:
                prev = jnp.where(first, 0.0, halo_ref[...].astype(F32))
            else:
                prev = ref[g0 - halo_ref.shape[0]:g0, :].astype(F32)
            prev = prev[prev.shape[0] - 1:, :]
            row = lax.broadcasted_iota(jnp.int32, cur.shape, 0)
            return jnp.where(row == 0, prev, pltpu.roll(cur, 1, 0))

        p = src["p"][rows, :].astype(F32)
        pm = p + (token_shift(p, src["p"], src["ph"]) - p) * src["mu"][...]
        yield
        k_raw = pm[:, MIX:2 * MIX]
        v = pm[:, 2 * MIX:3 * MIX]
        lora = pm[:, 3 * MIX:3 * MIX + LANES]
        g_lo = pm[:, 3 * MIX + LANES:3 * MIX + 2 * LANES]
        lane = lax.broadcasted_iota(jnp.int32, (gr, LANES), 1)
        z = jnp.where(lane < HEAD, jnp.tanh(lora), lora)
        wa = jnp.dot(z.astype(BF16), src["wa2"][...], preferred_element_type=F32)
        g_ref[rows, :] = jnp.dot(_sigmoid(g_lo).astype(BF16), src["g2"][...],
                                 preferred_element_type=F32).astype(BF16)
        r_ref[rows, :] = pm[:, 0:MIX].astype(BF16)
        yield
        w_raw = _log_sigmoid(src["w0"][...] + wa[:, :MIX]) - 0.5
        lw_ref[rows, :] = -jnp.exp(w_raw)
        a = _sigmoid(src["a0"][...] + wa[:, MIX:])
        a_ref[rows, :] = a.astype(BF16)
        k_ref[rows, :] = (k_raw * (1.0 + (a - 1.0) * src["k_a"][...])).astype(BF16)
        yield
        if has_vres:
            m = src["misc"][rows, :]
            ms = m + (token_shift(m, src["misc"], src["misch"]) - m) * src["vmu"][...]
            mix = _sigmoid(src["v0"][...] + jnp.dot(ms.astype(BF16), src["v2"][...], preferred_element_type=F32))
            v = v + (src["vf"][rows, :].astype(F32) - v) * mix
        else:
            vfo_ref[rows, :] = v.astype(BF16)
        v_ref[rows, :] = v.astype(BF16)
        kk = k_raw * src["k_k"][...]
        ss = _group_sum(kk * kk, e_ref[...])
        yield
        kk_ref[rows, :] = (kk * lax.rsqrt(jnp.maximum(ss, 1e-24))).astype(BF16)

    def prep(gi):
        rows = slice(gi * gr, (gi + 1) * gr)
        lw = lw_ref[rows, :]
        gam = _exact_dot(tri_ref[0:gr, 0:gr], lw)
        yield
        g_last = jnp.broadcast_to(gam.reshape(group, c, MIX)[:, c - 1:c, :], (group, c, MIX)).reshape(gr, MIX)
        kk = kk_ref[rows, :].astype(F32)
        kq_s[rows, :] = (kk * jnp.exp(gam - lw)).astype(BF16)
        rq_s[rows, :] = (r_ref[rows, :].astype(F32) * jnp.exp(gam)).astype(BF16)
        yield
        b = a_ref[rows, :].astype(F32) * kk
        k = k_ref[rows, :].astype(F32)
        e_inv = jnp.exp(-gam)
        bk_s[rows, :] = (b * e_inv).astype(BF16)
        kx_s[rows, :] = (k * e_inv).astype(BF16)
        yield
        e_dec = jnp.exp(g_last - gam)
        nb_s[rows, :] = (-(b * e_dec)).astype(BF16)
        kd_s[rows, :] = (k * e_dec).astype(BF16)
        dec_s[rows, :] = jnp.exp(g_last)
        yield

    lane = lax.broadcasted_iota(jnp.int32, (c, LANES), 1)
    t_idx = lax.broadcasted_iota(jnp.int32, (c, LANES), 0)
    s_idx = lane % HEAD
    left = lane < HEAD
    strict = t_idx > s_idx
    incl = t_idx >= s_idx
    eye = jnp.where(t_idx == s_idx, 1.0, 0.0)
    lane_r = lax.broadcasted_iota(jnp.int32, (LANES, LANES), 0) // HEAD
    lane_c = lax.broadcasted_iota(jnp.int32, (LANES, LANES), 1) // HEAD
    same_head = lane_r == lane_c

    def halves(x):
        zero = jnp.zeros_like(x)
        return jnp.where(left, x, zero), jnp.where(left, zero, x)

    def bdiag(x):
        x0, x1 = halves(x.astype(BF16))
        return jnp.concatenate([x0, x1], axis=0)

    def mm(a, rhs):
        return jnp.dot(a.astype(BF16), rhs, preferred_element_type=F32)


    def local_chain(start, p):
        rows = slice(start, start + c)
        sl = slice(p * LANES, (p + 1) * LANES)
        kq, rq = kq_s[rows, sl], rq_s[rows, sl]
        lhs = jnp.concatenate([kq, rq], axis=0)
        rhs = jnp.concatenate(halves(bk_s[rows, sl]) + halves(kx_s[rows, sl]), axis=0)
        aa = _dot_nt(lhs, rhs)
        yield
        a_ab = jnp.where(strict, aa[:c, :LANES], 0.0)
        a_ak = jnp.where(strict, aa[:c, LANES:], 0.0)
        a_rb = jnp.where(incl, aa[c:, :LANES], 0.0)
        a_rk = jnp.where(incl, aa[c:, LANES:], 0.0)
        yo = mm(jnp.concatenate([a_ak, a_rk], axis=0), bdiag(v_ref[rows, sl]))
        yield
        pw = -a_ab
        tinv = eye + pw
        pw = mm(pw, bdiag(pw))
        yield
        for _ in range(4):
            res = mm(pw, jnp.concatenate([bdiag(tinv), bdiag(pw)], axis=1))
            yield
            tinv = tinv + res[:, :LANES]
            pw = res[:, LANES:]
        tinv = tinv + mm(pw, bdiag(tinv))
        yield
        tw = mm(tinv, jnp.concatenate([bdiag(kq), bdiag(yo[:c])], axis=1))
        yield
        ar = mm(a_rb, jnp.concatenate([bdiag(tw[:, :LANES]), bdiag(tw[:, LANES:])], axis=1))
        yield
        tkq_s[rows, sl] = tw[:, :LANES].astype(BF16)
        w_s[rows, sl] = tw[:, LANES:]
        rqe_s[rows, sl] = (rq.astype(F32) - ar[:, :LANES]).astype(BF16)
        ol_s[rows, sl] = yo[c:] - ar[:, LANES:]

    def seq_group(gi):
        for u in range(group):
            start = (gi * group + u) * c
            rows = slice(start, start + c)
            states = [s_ref[p] for p in range(N_PAIRS)]
            xs = []
            for p in range(N_PAIRS):
                sl = slice(p * LANES, (p + 1) * LANES)
                lhs = jnp.concatenate([tkq_s[rows, sl], rqe_s[rows, sl]], axis=0)
                xs.append(_dot_nt(lhs, states[p].astype(BF16)))
            yield
            upds = []
            for p in range(N_PAIRS):
                sl = slice(p * LANES, (p + 1) * LANES)
                uu = xs[p][:c] + w_s[rows, sl]
                oo_s[rows, sl] = xs[p][c:] + ol_s[rows, sl]
                wv = jnp.concatenate([uu.astype(BF16), v_ref[rows, sl]], axis=0)
                wk = jnp.concatenate([nb_s[rows, sl], kd_s[rows, sl]], axis=0)
                upds.append(_dot_tn(wv, wk))
            yield
            for p in range(N_PAIRS):
                sl = slice(p * LANES, (p + 1) * LANES)
                s_ref[p] = states[p] * dec_s[start:start + 1, sl] + jnp.where(same_head, upds[p], 0.0)

    def local_group(gi):
        return [local_chain((gi * group + u) * c, p) for u in range(group) for p in range(N_PAIRS)]

    def post(gi):
        rows = slice(gi * gr, (gi + 1) * gr)
        e = e_ref[...]
        o = oo_s[rows, :]
        mean = _group_sum2(o, e) * (1.0 / HEAD)
        yield
        d = o - mean
        var = _group_sum2(d * d, e) * (1.0 / HEAD)
        yield
        v = v_ref[rows, :].astype(F32)
        bonus = _group_sum2(r_ref[rows, :].astype(F32) * k_ref[rows, :].astype(F32) * rk_ref[...], e)
        yield
        out = d * lax.rsqrt(var + RWKV_LN_EPS) * lnw_ref[...] + lnb_ref[...] + bonus * v
        o_ref[rows, :] = (out * g_ref[rows, :].astype(F32)).astype(o_ref.dtype)

    for step in range(n_groups + 4):
        chains = []
        if step < n_groups:
            chains.append(proj(step))
        if 0 <= step - 1 < n_groups:
            chains.append(prep(step - 1))
        if 0 <= step - 2 < n_groups:
            chains += local_group(step - 2)
        if 0 <= step - 3 < n_groups:
            chains.append(seq_group(step - 3))
        if 0 <= step - 4 < n_groups:
            chains.append(post(step - 4))
        yield from _lockstep_rounds(chains)


HGRN_TILE = 256

_RWKV_SRC = ("p", "ph", "mu", "w0", "a0", "wa2", "g2", "k_k", "k_a")
_RWKV_SRC_VRES = ("misc", "misch", "vmu", "v0", "v2", "vf")


def _mixers_kernel(*refs, has_vres):
    names = _RWKV_SRC + (_RWKV_SRC_VRES if has_vres else ())
    refs = list(refs)
    hgrn_in, refs = refs[:8], refs[8:]
    src, refs = dict(zip(names, refs[:len(names)])), refs[len(names):]
    (rk_ref, lnw_ref, lnb_ref, tri64_ref), refs = refs[:4], refs[4:]
    if has_vres:
        (oh_ref, or_ref), refs = refs[:2], refs[2:]
        vfo_ref = None
    else:
        (oh_ref, or_ref, vfo_ref), refs = refs[:3], refs[3:]
    hgrn_scratch, rwkv_scratch = refs[:6], refs[6:]
    e_ref = hgrn_in[6]

    @pl.when(pl.program_id(1) == 0)
    def _():
        hgrn_scratch[0][...] = jnp.zeros_like(hgrn_scratch[0])
        rwkv_scratch[7][...] = jnp.zeros_like(rwkv_scratch[7])

    def hgrn_all():
        for row0 in range(0, oh_ref.shape[0], HGRN_TILE):
            yield from _hgrn_stages(*hgrn_in, oh_ref, *hgrn_scratch, row0)

    rwkv_all = _rwkv_stages(src, rk_ref, lnw_ref, lnb_ref, tri64_ref, e_ref, or_ref, vfo_ref, *rwkv_scratch)
    hgrn = hgrn_all()
    n_h, n_r, tick, done_h = 3, 5, 0, False
    for _ in rwkv_all:
        if not done_h and ((tick + 1) * n_h) // n_r > (tick * n_h) // n_r:
            done_h = next(hgrn, "done") == "done"
        tick += 1
    for _ in hgrn:
        pass


def _mixers(main, aux, lower, o_gain, prm, v_first, r_k, ln_w, ln_b, bsz, seq, tm):
    nt = seq // tm
    has_vres = v_first is not None
    ch = COL_HGRN // MIX
    pw = 4 * MIX
    cp = COL_RWKV // pw
    halo = 16
    row = lambda cidx: (lambda b, i: (b * nt + i, cidx))
    const = lambda b, i: (0, 0)

    def halo_map(rows, cidx):
        per = tm // rows
        return lambda b, i: (jnp.maximum((b * nt + i) * per - 1, 0), cidx)

    spec = pl.BlockSpec((tm, MIX), row(0))
    vec = pl.BlockSpec((1, MIX), const)
    in_specs = [pl.BlockSpec((tm, MIX), row(ch)), pl.BlockSpec((tm, MIX), row(ch + 1)),
                pl.BlockSpec((tm, MIX), row(ch + 2)), spec, vec, vec,
                pl.BlockSpec((MIX // 2, MIX // 2), const),
                pl.BlockSpec((HGRN_TILE, HGRN_TILE), const),
                pl.BlockSpec((tm, pw), row(cp)),
                pl.BlockSpec((halo, pw), halo_map(halo, cp)),
                pl.BlockSpec((1, pw), const), vec, vec,
                pl.BlockSpec((LANES, 2 * MIX), const),
                pl.BlockSpec((LANES, MIX), const), vec, vec]
    args = [main, main, main, aux, lower, o_gain, _block_ones(MIX // 2), _block_tril(HGRN_TILE, HGRN_SUB),
            main, main, prm["mu"], prm["w0"], prm["a0"], prm["wa2"], prm["g2"], prm["k_k"], prm["k_a"]]
    if has_vres:
        in_specs += [pl.BlockSpec((tm, LANES), row(MIX // LANES)),
                     pl.BlockSpec((8, LANES), halo_map(8, MIX // LANES)),
                     pl.BlockSpec((1, LANES), const), vec,
                     pl.BlockSpec((LANES, MIX), const), spec]
        args += [aux, aux, prm["vmu"], prm["v0"], prm["v2"], v_first]
    in_specs += [vec, vec, vec, pl.BlockSpec((tm, tm), const)]
    args += [r_k, ln_w, ln_b, _block_tril(tm, RWKV_CHUNK)]
    h16 = pltpu.VMEM((HGRN_TILE, MIX), BF16)
    h32 = pltpu.VMEM((HGRN_TILE, MIX), F32)
    s16 = pltpu.VMEM((tm, MIX), BF16)
    s32 = pltpu.VMEM((tm, MIX), F32)
    state = pltpu.VMEM((N_PAIRS, LANES, LANES), F32)
    out = jax.ShapeDtypeStruct((bsz * seq, MIX), BF16)
    n_out = 2 if has_vres else 3
    res = pl.pallas_call(
        functools.partial(_mixers_kernel, has_vres=has_vres),
        grid=(bsz, nt),
        in_specs=in_specs,
        out_specs=[spec] * n_out,
        out_shape=[out] * n_out,
        scratch_shapes=[state, h16, h16, h16, h32, h32,
                        s16, s16, s16, s16, s16, s32, s16,
                        state, s16, s16, s16, s16, s16, s16, s32, s16, s16, s32, s32, s32],
        compiler_params=_cparams(("parallel", "arbitrary")),
        name="hgrn_rwkv",
    )(*args)
    return (res[0], res[1], v_first) if has_vres else tuple(res)


def _merge_kernel(yf_ref, yh_ref, yr_ref, gf_ref, gh_ref, gr_ref, x_ref, gt_ref, wb_ref, wo_ref, o_ref):
    def branch(y_ref, g_ref, i):
        return _sigmoid(g_ref[...].astype(F32)) * jnp.dot(y_ref[...], wb_ref[i], preferred_element_type=F32)

    merged = branch(yf_ref, gf_ref, 0) + branch(yh_ref, gh_ref, 1) + branch(yr_ref, gr_ref, 2)
    o_ref[...] = x_ref[...] + gt_ref[0] * jnp.dot(merged.astype(BF16), wo_ref[...], preferred_element_type=F32)


def _merge(y_fox, y_hgrn, y_rwkv, main, x, gt, w_branch, w_out, seq, tm):
    t, d = x.shape
    tps = seq // tm
    yspec = pl.BlockSpec((tm, MIX), lambda i: (i, 0))
    gspec = lambda cidx: pl.BlockSpec((tm, d), lambda i: (i, COL_GATE // d + cidx))
    return pl.pallas_call(
        _merge_kernel,
        grid=(t // tm,),
        in_specs=[yspec, yspec, yspec, gspec(0), gspec(1), gspec(2),
                  pl.BlockSpec((tm, d), lambda i: (i, 0)),
                  pl.BlockSpec((1, 1, d), lambda i: (i // tps, 0, 0)),
                  pl.BlockSpec((3, MIX, d), lambda i: (0, 0, 0)),
                  pl.BlockSpec((d, d), lambda i: (0, 0))],
        out_specs=pl.BlockSpec((tm, d), lambda i: (i, 0)),
        out_shape=jax.ShapeDtypeStruct((t, d), F32),
        compiler_params=_cparams(("parallel",)),
        name="merge_out",
    )(y_fox, y_hgrn, y_rwkv, main, main, main, x, gt, w_branch, w_out)


FFN_HALO = 16
FFN_TF = 256


def _ffn_body(x_ref, xh_ref, sc_ref, sh_ref, gt_ref, wup_ref, cw_ref, cb_ref, wd_ref, o_ref, h_ref, act_ref,
              *, tps):
    def normed(x):
        ms = jnp.mean(x * x, axis=-1, keepdims=True)
        return x * lax.rsqrt(ms + NORM_EPS) * sc_ref[0] + sh_ref[0]

    seq_start = pl.program_id(0) % tps == 0
    h_ref[0:FFN_HALO, :] = jnp.where(seq_start, 0.0, normed(xh_ref[...])).astype(BF16)
    h_ref[FFN_HALO:, :] = normed(x_ref[...]).astype(BF16)

    def up_proj(col):
        return jnp.dot(h_ref[...], wup_ref[:, col:col + FFN_TF], preferred_element_type=F32)

    def conv(u, col):
        cw = cw_ref[:, col:col + FFN_TF]
        y = cb_ref[:, col:col + FFN_TF] + cw[0:1] * pltpu.roll(u, 2, 0) + cw[1:2] * pltpu.roll(u, 1, 0) + cw[2:3] * u
        return y[FFN_HALO:]

    cols = [base + j * FFN_TF for j in range(D_FF // FFN_TF) for base in (0, D_FF)]
    u_next = up_proj(cols[0])
    val = None
    for i, col in enumerate(cols):
        u = u_next
        if i + 1 < len(cols):
            u_next = up_proj(cols[i + 1])
        y = conv(u, col)
        if i % 2 == 0:
            val = y
        else:
            act_ref[:, col - D_FF:col - D_FF + FFN_TF] = (y * _sigmoid(y) * val).astype(BF16)

    down = jnp.dot(act_ref[...], wd_ref[...], preferred_element_type=F32)
    o_ref[...] = x_ref[...] + gt_ref[0] * down


def _ffn(x, scale, shift, gt, w_up, conv_w, conv_b, w_down, seq, tm):
    t, d = x.shape
    tps = seq // tm
    per = tm // FFN_HALO
    bmap = lambda i: (i // tps, 0, 0)
    const = lambda i: (0, 0)
    resident = dict(pipeline_mode=pl.Buffered(1))
    return pl.pallas_call(
        functools.partial(_ffn_body, tps=tps),
        grid=(t // tm,),
        in_specs=[pl.BlockSpec((tm, d), lambda i: (i, 0)),
                  pl.BlockSpec((FFN_HALO, d), lambda i: (jnp.maximum(i * per - 1, 0), 0)),
                  pl.BlockSpec((1, 1, d), bmap),
                  pl.BlockSpec((1, 1, d), bmap),
                  pl.BlockSpec((1, 1, d), bmap),
                  pl.BlockSpec((d, 2 * D_FF), const, **resident),
                  pl.BlockSpec((3, 2 * D_FF), const),
                  pl.BlockSpec((1, 2 * D_FF), const),
                  pl.BlockSpec((D_FF, d), const, **resident)],
        out_specs=pl.BlockSpec((tm, d), lambda i: (i, 0)),
        out_shape=jax.ShapeDtypeStruct((t, d), F32),
        scratch_shapes=[pltpu.VMEM((tm + FFN_HALO, d), BF16), pltpu.VMEM((tm, D_FF), BF16)],
        compiler_params=_cparams(("parallel",)),
        name="conv_mlp",
    )(x, x, scale, shift, gt, w_up, conv_w, conv_b, w_down)


def _pack_in_proj(w_in, vres_down):
    depth, d, _ = w_in.shape
    fox, hgrn, rwkv, gate = 0, 1544, 3592, 5384
    w_in = w_in.astype(BF16)
    vres_down = vres_down.astype(BF16)
    z = lambda n: jnp.zeros((depth, d, n), w_in.dtype)
    main = jnp.concatenate([
        w_in[..., gate:gate + 3072],
        w_in[..., fox:fox + 1536],
        w_in[..., hgrn:hgrn + 512], w_in[..., hgrn + 1024:hgrn + 2048],
        w_in[..., rwkv:rwkv + 1792],
        z(N_MAIN - 7936)], axis=-1)
    vres = jnp.concatenate([jnp.zeros((1, d, VRES_RANK), w_in.dtype), vres_down], axis=0)
    aux = jnp.concatenate([
        w_in[..., hgrn + 512:hgrn + 1024],
        w_in[..., fox + 1536:fox + 1544], z(AUX_VRES_LANE - 8),
        vres, z(LANES - AUX_VRES_LANE - VRES_RANK)], axis=-1)
    assert main.shape[-1] == N_MAIN and aux.shape[-1] == N_AUX
    return main, aux


def _tile_heads(v):
    return jnp.tile(v, N_HEADS).reshape(1, MIX)


def _forward(x, c, w_ada, b_ada, norm1_g, norm2_g, w_in, fox_b_f, fox_q_gain, fox_k_gain,
             hgrn_lb, hgrn_o_gain, rwkv_mu, rwkv_w0, rwkv_w2, rwkv_a0, rwkv_a2, rwkv_g2,
             rwkv_k_k, rwkv_k_a, rwkv_r_k, rwkv_ln_w, rwkv_ln_b, rwkv_vres_down, rwkv_vres_mu,
             rwkv_v0, rwkv_v2, w_branch, w_out, w_up, conv_w, conv_b, w_down, *, tiles):
    bsz, seq, d = x.shape
    depth = w_in.shape[0]
    t = bsz * seq

    lb_prob = jax.nn.softmax(hgrn_lb.astype(F32), axis=0)
    hgrn_lower = jnp.cumsum(lb_prob, axis=0) - lb_prob[0]

    mod = _ada_mod(c, w_ada, b_ada)
    w_main, w_aux = _pack_in_proj(w_in, rwkv_vres_down)
    w_branch16, w_out16 = w_branch.astype(BF16), w_out.astype(BF16)
    w_up16, w_down16 = w_up.astype(BF16), w_down.astype(BF16)

    xf = x.reshape(t, d)
    v_first = None
    for l in range(depth):
        sh1, sc1, gt1, sh2, sc2, gt2 = [m.reshape(bsz, 1, d) for m in jnp.split(mod[l], 6, axis=-1)]
        scale1 = norm1_g[l] * (1.0 + sc1)
        b_f = jnp.zeros((1, LANES), F32).at[0, AUX_FF_LANE:AUX_FF_LANE + N_HEADS].set(fox_b_f[l])
        main, aux, qh, kh, vh = _in_proj(xf, scale1, sh1, w_main[l], w_aux[l],
                                         _tile_heads(fox_q_gain[l]) * HEAD ** -0.5, _tile_heads(fox_k_gain[l]), b_f,
                                         bsz, seq, tiles["proj_tm"], tiles["proj_tn"])
        y_fox = _fox_attn(qh, kh, vh, tiles["fox_tq"], tiles["fox_sub"], tiles["fox_tk"])

        wa2 = jnp.zeros((LANES, 2 * MIX), F32)
        wa2 = wa2.at[:HEAD, :MIX].set(rwkv_w2[l]).at[HEAD:, MIX:].set(rwkv_a2[l])
        prm = {
            "mu": jnp.concatenate([rwkv_mu[l], jnp.zeros((4 * MIX - rwkv_mu.shape[1],), F32)]).reshape(1, 4 * MIX),
            "w0": rwkv_w0[l].reshape(1, MIX), "a0": rwkv_a0[l].reshape(1, MIX),
            "wa2": wa2.astype(BF16), "g2": rwkv_g2[l].astype(BF16),
            "k_k": rwkv_k_k[l].reshape(1, MIX), "k_a": rwkv_k_a[l].reshape(1, MIX),
        }
        if l > 0:
            vmu = jnp.zeros((1, LANES), F32).at[0, AUX_VRES_LANE:AUX_VRES_LANE + VRES_RANK].set(rwkv_vres_mu[l - 1])
            v2 = jnp.zeros((LANES, MIX), F32).at[AUX_VRES_LANE:AUX_VRES_LANE + VRES_RANK].set(rwkv_v2[l - 1])
            prm.update(vmu=vmu, v0=rwkv_v0[l - 1].reshape(1, MIX), v2=v2.astype(BF16))
        y_hgrn, y_rwkv, v_first = _mixers(main, aux, hgrn_lower[l].reshape(1, MIX), _tile_heads(hgrn_o_gain[l]),
                                          prm, v_first, rwkv_r_k[l].reshape(1, MIX), rwkv_ln_w[l].reshape(1, MIX),
                                          rwkv_ln_b[l].reshape(1, MIX), bsz, seq, tiles["rwkv_tm"])

        xf = _merge(y_fox, y_hgrn, y_rwkv, main, xf, gt1, w_branch16[l], w_out16[l], seq, tiles["merge_tm"])

        scale2 = norm2_g[l] * (1.0 + sc2)
        xf = _ffn(xf, scale2, sh2, gt2, w_up16[l], conv_w[l], conv_b[l].reshape(1, -1), w_down16[l], seq,
                  tiles["ffn_tm"])
    return xf.reshape(bsz, seq, d)


def _tiles_for(seq):
    cap = lambda n: min(n, seq)
    return dict(proj_tm=cap(1024), proj_tn=1024, fox_tq=cap(1024), fox_sub=256, fox_tk=cap(1024),
                rwkv_tm=cap(512), merge_tm=cap(1024), ffn_tm=cap(1024))


def kernel(x, c, w_ada, b_ada, norm1_g, norm2_g, w_in, fox_b_f, fox_q_gain, fox_k_gain, hgrn_lb, hgrn_o_gain, rwkv_mu, rwkv_w0, rwkv_w2, rwkv_a0, rwkv_a2, rwkv_g2, rwkv_k_k, rwkv_k_a, rwkv_r_k, rwkv_ln_w, rwkv_ln_b, rwkv_vres_down, rwkv_vres_mu, rwkv_v0, rwkv_v2, w_branch, w_out, w_up, conv_w, conv_b, w_down):
    return _forward(x, c, w_ada, b_ada, norm1_g, norm2_g, w_in, fox_b_f, fox_q_gain, fox_k_gain,
                    hgrn_lb, hgrn_o_gain, rwkv_mu, rwkv_w0, rwkv_w2, rwkv_a0, rwkv_a2, rwkv_g2,
                    rwkv_k_k, rwkv_k_a, rwkv_r_k, rwkv_ln_w, rwkv_ln_b, rwkv_vres_down, rwkv_vres_mu,
                    rwkv_v0, rwkv_v2, w_branch, w_out, w_up, conv_w, conv_b, w_down,
                    tiles=_tiles_for(x.shape[1]))
```

```python
import functools

import jax
import jax.numpy as jnp
import numpy as np
from jax import lax
from jax.experimental import pallas as pl
from jax.experimental.pallas import tpu as pltpu

F32 = jnp.float32
BF16 = jnp.bfloat16

MIX = 512
HEAD = 64
N_HEADS = MIX // HEAD
N_PAIRS = N_HEADS // 2
LANES = 128
D_FF = 2816
NORM_EPS = 1e-6
RWKV_LN_EPS = 64e-5
MASK_VALUE = -1e30
LOG_FLOOR = 1e-30
LOG2E = 1.4426950408889634
VRES_RANK = 32

HGRN_SUB = 16
RWKV_CHUNK = 64

COL_GATE = 0
COL_FOX = 3072
COL_HGRN = 4608
COL_RWKV = 6144
N_MAIN = 8192
N_AUX = 640
AUX_FF_LANE = 0
AUX_VRES_LANE = 32

VMEM_LIMIT = 56 * 1024 * 1024


def _cparams(sem):
    return pltpu.CompilerParams(dimension_semantics=sem, vmem_limit_bytes=VMEM_LIMIT)


def _split3(x):
    hi = x.astype(BF16)
    r1 = x - hi.astype(F32)
    mid = r1.astype(BF16)
    lo = (r1 - mid.astype(F32)).astype(BF16)
    return hi, mid, lo


def _exact_dot(a01, x):
    hi, mid, lo = _split3(x)
    d = lambda b: jnp.dot(a01, b, preferred_element_type=F32)
    return d(hi) + d(mid) + d(lo)


def _group_sum(x, e):
    w = e.shape[0]
    xb = x.astype(BF16)
    if x.shape[1] == w:
        return jnp.dot(xb, e, preferred_element_type=F32)
    parts = [jnp.dot(xb[:, i:i + w], e, preferred_element_type=F32) for i in range(0, x.shape[1], w)]
    return jnp.concatenate(parts, axis=1)


def _group_sum2(x, e):
    hi = x.astype(BF16)
    lo = x - hi.astype(F32)
    return _group_sum(hi, e) + _group_sum(lo, e)


def _sigmoid(x):
    return 1.0 / (1.0 + jnp.exp(-x))


def _log_sigmoid(x):
    return jnp.minimum(x, 0.0) - jnp.log1p(jnp.exp(-jnp.abs(x)))


def _dot_nt(a, b):
    return lax.dot_general(a, b, (((1,), (1,)), ((), ())), preferred_element_type=F32)


def _dot_tn(a, b):
    return lax.dot_general(a, b, (((0,), (0,)), ((), ())), preferred_element_type=F32)


def _run_lockstep(gens):
    live = list(gens)
    while live:
        nxt = []
        for g in live:
            try:
                next(g)
                nxt.append(g)
            except StopIteration:
                pass
        live = nxt


def _lockstep_rounds(gens):
    live = list(gens)
    while live:
        nxt = []
        for g in live:
            try:
                next(g)
                nxt.append(g)
            except StopIteration:
                pass
        live = nxt
        yield


def _block_ones(width):
    g = np.arange(width) // HEAD
    return jnp.asarray(g[:, None] == g[None, :], dtype=BF16)


def _block_tril(n, blk):
    i = np.arange(n)
    m = (i[:, None] >= i[None, :]) & ((i[:, None] // blk) == (i[None, :] // blk))
    return jnp.asarray(m, dtype=BF16)


def _ada_kernel(c_ref, w_ref, b_ref, o_ref):
    c = c_ref[...]
    cond = c * _sigmoid(c)
    o_ref[0] = jnp.dot(cond.astype(BF16), w_ref[0].astype(BF16), preferred_element_type=F32) + b_ref[0]


def _ada_mod(c, w_ada, b_ada, tn=1536):
    depth, d, n = w_ada.shape
    bsz = c.shape[0]
    return pl.pallas_call(
        _ada_kernel,
        grid=(depth, n // tn),
        in_specs=[pl.BlockSpec((bsz, d), lambda l, j: (0, 0)),
                  pl.BlockSpec((1, d, tn), lambda l, j: (l, 0, j)),
                  pl.BlockSpec((1, 1, tn), lambda l, j: (l, 0, j))],
        out_specs=pl.BlockSpec((1, bsz, tn), lambda l, j: (l, 0, j)),
        out_shape=jax.ShapeDtypeStruct((depth, bsz, n), F32),
        compiler_params=_cparams(("parallel", "parallel")),
        name="ada_mod",
    )(c, w_ada, b_ada.reshape(depth, 1, n))


FOX_PREP_ROWS = 512


def _fox_prep_stages(qkv_s, misc_s, qg_ref, kg_ref, bf_ref, e_ref, tri_ref, sel_ref,
                     qo_ref, ko_ref, vo_ref, carry_ref, row0):
    tm = FOX_PREP_ROWS
    blk = slice(row0, row0 + tm)
    e = e_ref[...]

    def qk_norm(cols, gain):
        xx = qkv_s[blk, cols].astype(F32)
        ss = _group_sum(xx * xx, e)
        return xx * lax.rsqrt(ss * (1.0 / HEAD) + NORM_EPS) * gain

    qn = qk_norm(slice(0, MIX), qg_ref[...])
    yield
    kn = qk_norm(slice(MIX, 2 * MIX), kg_ref[...])
    vv = qkv_s[blk, 2 * MIX:3 * MIX].astype(F32)
    yield

    lf = _log_sigmoid(misc_s[blk, :] + bf_ref[...])
    cum = _exact_dot(tri_ref[...], lf) + carry_ref[...]
    carry_ref[...] = cum[tm - 1:tm, :]
    yield
    hi, mid, lo = [s.astype(F32) for s in _split3(cum)]
    lane = lax.broadcasted_iota(jnp.int32, (tm, LANES), 1)
    fcat = jnp.where(lane < 8, hi, jnp.where(lane < 16, pltpu.roll(mid, 8, 1),
                     jnp.where(lane < 24, pltpu.roll(lo, 16, 1), jnp.where(lane == 24, 1.0, 0.0))))
    ext = jnp.dot(fcat.astype(BF16), sel_ref[...], preferred_element_type=F32)
    yield

    for p in range(N_PAIRS):
        sl = slice(p * LANES, (p + 1) * LANES)
        qp, kp, vp = qn[:, sl], kn[:, sl], vv[:, sl]
        for par in range(2):
            h = 2 * p + par
            base = HEAD if par == 0 else 0
            keep = (lane < HEAD) if par == 0 else (lane >= HEAD)
            qo_ref[0, h, :, blk] = jnp.where(keep, qp, ext[:, 2 * h * LANES:(2 * h + 1) * LANES]).T.astype(BF16)
            ko_ref[0, h, blk, :] = jnp.where(keep, kp, ext[:, (2 * h + 1) * LANES:(2 * h + 2) * LANES]).astype(BF16)
            vo_ref[0, h, :, blk] = jnp.where(keep, vp, jnp.where(lane == base, 1.0, 0.0)).T.astype(BF16)
        yield


def _fox_sel():
    sel = np.zeros((LANES, N_HEADS, 2, LANES), np.float32)
    for h in range(N_HEADS):
        base = HEAD if h % 2 == 0 else 0
        for i in range(3):
            sel[8 * i + h, h, 0, base + i] = 1.0
            sel[24, h, 0, base + 3 + i] = 1.0
            sel[24, h, 1, base + i] = 1.0
            sel[8 * i + h, h, 1, base + 3 + i] = -1.0
    return jnp.asarray(sel.reshape(LANES, N_HEADS * 2 * LANES), dtype=BF16)


def _in_proj_kernel(x_ref, sc_ref, sh_ref, w_ref, wa_ref, qg_ref, kg_ref, bf_ref, e_ref, tri_ref, sel_ref,
                    o_ref, oa_ref, qo_ref, ko_ref, vo_ref, h_ref, qkv_s, misc_s, carry_ref, *, n_main, tps):
    j = pl.program_id(1)
    tm, tn = o_ref.shape
    fox_blk = COL_FOX // tn + 1
    prep_rows = list(range(0, tm, FOX_PREP_ROWS))
    prep_steps = [fox_blk + 2 + u for u in range(len(prep_rows))]

    @pl.when(j == 0)
    def _():
        x = x_ref[...]
        ms = jnp.mean(x * x, axis=-1, keepdims=True)
        h_ref[...] = (x * lax.rsqrt(ms + NORM_EPS) * sc_ref[0] + sh_ref[0]).astype(BF16)
        aux = jnp.dot(h_ref[...], wa_ref[...], preferred_element_type=F32)
        oa_ref[...] = aux
        misc_s[...] = aux[:, MIX:MIX + LANES]

        @pl.when(pl.program_id(0) % tps == 0)
        def _():
            carry_ref[...] = jnp.zeros_like(carry_ref)

    def block():
        return jnp.dot(h_ref[...], w_ref[...], preferred_element_type=F32).astype(BF16)

    plain = (j >= 1) & (j != fox_blk) & (j != fox_blk + 1)
    for st in prep_steps:
        plain = plain & (j != st)

    @pl.when(plain)
    def _():
        o_ref[...] = block()

    @pl.when(j == fox_blk)
    def _():
        o = block()
        o_ref[...] = o
        qkv_s[:, 0:2 * MIX] = o[:, 0:2 * MIX]

    @pl.when(j == fox_blk + 1)
    def _():
        o = block()
        o_ref[...] = o
        qkv_s[:, 2 * MIX:3 * MIX] = o[:, 0:MIX]

    for st, row0 in zip(prep_steps, prep_rows):
        @pl.when(j == st)
        def _(row0=row0):
            def matmul():
                sub = tm // 4
                for r in range(0, tm, sub):
                    o_ref[r:r + sub, :] = jnp.dot(h_ref[r:r + sub, :], w_ref[...],
                                                  preferred_element_type=F32).astype(BF16)
                    yield

            _run_lockstep([matmul(), _fox_prep_stages(qkv_s, misc_s, qg_ref, kg_ref, bf_ref, e_ref, tri_ref,
                                                       sel_ref, qo_ref, ko_ref, vo_ref, carry_ref, row0)])


def _in_proj(x, scale, shift, w, w_aux, q_gain, k_gain, b_f, bsz, seq, tm, tn):
    t, d = x.shape
    n = w.shape[1]
    n_aux = w_aux.shape[1]
    n_main = n // tn
    tps = seq // tm
    assert COL_FOX % tn == 0 and 2 * MIX == tn and tm % FOX_PREP_ROWS == 0
    assert COL_FOX // tn + 3 + tm // FOX_PREP_ROWS <= n_main + 1
    col = lambda i, j: (0, jnp.maximum(j - 1, 0))
    const = lambda i, j: (0, 0)
    head = lambda i, j: (i // tps, 0, i % tps, 0)
    head_t = lambda i, j: (i // tps, 0, 0, i % tps)
    out = jax.ShapeDtypeStruct((bsz, N_HEADS, seq, LANES), BF16)
    out_t = jax.ShapeDtypeStruct((bsz, N_HEADS, LANES, seq), BF16)
    return pl.pallas_call(
        functools.partial(_in_proj_kernel, n_main=n_main, tps=tps),
        grid=(t // tm, n_main + 1),
        in_specs=[pl.BlockSpec((tm, d), lambda i, j: (i, 0)),
                  pl.BlockSpec((1, 1, d), lambda i, j: (i // tps, 0, 0)),
                  pl.BlockSpec((1, 1, d), lambda i, j: (i // tps, 0, 0)),
                  pl.BlockSpec((d, tn), col),
                  pl.BlockSpec((d, n_aux), const),
                  pl.BlockSpec((1, MIX), const),
                  pl.BlockSpec((1, MIX), const),
                  pl.BlockSpec((1, LANES), const),
                  pl.BlockSpec((MIX // 2, MIX // 2), const),
                  pl.BlockSpec((FOX_PREP_ROWS, FOX_PREP_ROWS), const),
                  pl.BlockSpec((LANES, N_HEADS * 2 * LANES), const)],
        out_specs=[pl.BlockSpec((tm, tn), lambda i, j: (i, jnp.maximum(j - 1, 0))),
                   pl.BlockSpec((tm, n_aux), lambda i, j: (i, 0)),
                   pl.BlockSpec((1, N_HEADS, LANES, tm), head_t),
                   pl.BlockSpec((1, N_HEADS, tm, LANES), head),
                   pl.BlockSpec((1, N_HEADS, LANES, tm), head_t)],
        out_shape=[jax.ShapeDtypeStruct((t, n), BF16), jax.ShapeDtypeStruct((t, n_aux), F32), out_t, out, out_t],
        scratch_shapes=[pltpu.VMEM((tm, d), BF16), pltpu.VMEM((tm, 3 * MIX), BF16), pltpu.VMEM((tm, LANES), F32),
                        pltpu.VMEM((1, LANES), F32)],
        compiler_params=_cparams(("arbitrary", "arbitrary")),
        name="in_proj",
    )(x, scale, shift, w, w_aux, q_gain, k_gain, b_f, _block_ones(MIX // 2),
      _block_tril(FOX_PREP_ROWS, FOX_PREP_ROWS), _fox_sel())


def _fox_attn_kernel(qt_ref, k_ref, vt_ref, o_ref, *, tq, sub, tk):
    qi = pl.program_id(2)
    n_sub = tq // sub
    kpt = tq // tk
    chains = [(hh, r) for hh in range(2) for r in range(n_sub)]

    def step(j, carry, key_off):
        new = list(carry)
        start = pl.multiple_of(j * tk, tk)

        def chain(idx, hh, r):
            q_lo = r * sub
            nk = tk if key_off is None else min(q_lo + sub - key_off, tk)
            if nk <= 0:
                return
            if idx % 2 == 1:
                yield
            m, acc = carry[idx]
            qt = qt_ref[0, hh, :, q_lo:q_lo + sub]
            kb = k_ref[0, hh, pl.ds(start, nk), :]
            vtb = vt_ref[0, hh, :, pl.ds(start, nk)]
            st = jnp.dot(kb, qt, preferred_element_type=F32)
            yield
            if key_off is not None and key_off + nk > q_lo:
                kpos = lax.broadcasted_iota(jnp.int32, (nk, sub), 0) + key_off
                qpos = lax.broadcasted_iota(jnp.int32, (nk, sub), 1) + q_lo
                st = jnp.where(kpos <= qpos, st, MASK_VALUE)
            m_new = jnp.maximum(m, jnp.max(st, axis=0, keepdims=True))
            p = jnp.exp(st - m_new).astype(BF16)
            yield
            pv = jnp.dot(vtb, p, preferred_element_type=F32)
            yield
            new[idx] = (m_new, jnp.exp(m - m_new) * acc + pv)

        _run_lockstep([chain(i, hh, r) for i, (hh, r) in enumerate(chains)])
        return tuple(new)

    init = tuple((jnp.full((1, sub), MASK_VALUE, F32), jnp.zeros((LANES, sub), F32)) for _ in chains)
    carry = lax.fori_loop(0, qi * kpt, lambda j, c: step(j, c, None), init)
    for jd in range(kpt):
        carry = step(qi * kpt + jd, carry, jd * tk)
    row = lax.broadcasted_iota(jnp.int32, (LANES, tq), 0)
    accs = [jnp.concatenate([carry[hh * n_sub + r][1] for r in range(n_sub)], axis=1) for hh in range(2)]
    ot = jnp.where(row < HEAD, accs[0] / accs[0][HEAD:HEAD + 1, :], accs[1] / accs[1][0:1, :])
    o_ref[...] = ot.T.astype(o_ref.dtype)


def _fox_attn(qt, kh, vt, tq, sub, tk):
    bsz, _, seq, _ = kh.shape
    nq = seq // tq
    return pl.pallas_call(
        functools.partial(_fox_attn_kernel, tq=tq, sub=sub, tk=tk),
        grid=(bsz, N_PAIRS, nq),
        in_specs=[pl.BlockSpec((1, 2, LANES, tq), lambda b, p, i: (b, p, 0, i)),
                  pl.BlockSpec((1, 2, seq, LANES), lambda b, p, i: (b, p, 0, 0)),
                  pl.BlockSpec((1, 2, LANES, seq), lambda b, p, i: (b, p, 0, 0))],
        out_specs=pl.BlockSpec((tq, LANES), lambda b, p, i: (b * nq + i, p)),
        out_shape=jax.ShapeDtypeStruct((bsz * seq, MIX), BF16),
        compiler_params=_cparams(("parallel", "parallel", "arbitrary")),
        name="fox_attn",
    )(qt, kh, vt)


def _hgrn_stages(q_ref, i_ref, og_ref, f_ref, lb_ref, gain_ref, e_ref, tri_ref, o_ref,
                 s_ref, qd_ref, kd_ref, vb_ref, ge_ref, os_ref, row0):
    tm = qd_ref.shape[0]
    c = HGRN_SUB
    n = tm // c
    blk = slice(row0, row0 + tm)

    lb = lb_ref[...]
    gate = lb + (1.0 - lb) * _sigmoid(f_ref[blk, :])
    ell = jnp.log(jnp.maximum(gate, LOG_FLOOR))
    kx = 1.0 - gate
    qraw = q_ref[blk, :].astype(F32)
    q = qraw * _sigmoid(qraw)
    v = i_ref[blk, :].astype(F32)
    g = _exact_dot(tri_ref[...], ell)

    def row_of_chunk(a, j):
        a3 = a.reshape(n, c, MIX)
        return jnp.broadcast_to(a3[:, j:j + 1, :], (n, c, MIX)).reshape(tm, MIX)

    g_end = row_of_chunk(g, c - 1)
    qd_ref[...] = (q * jnp.exp(g)).astype(BF16)
    kd_ref[...] = (kx * jnp.exp(g_end - g)).astype(BF16)
    vb_ref[...] = v.astype(BF16)
    ge_ref[...] = g_end

    e = e_ref[...]
    pos = lax.broadcasted_iota(jnp.int32, (tm, MIX), 0) % c
    lane_r = lax.broadcasted_iota(jnp.int32, (LANES, LANES), 0) // HEAD
    lane_c = lax.broadcasted_iota(jnp.int32, (LANES, LANES), 1) // HEAD
    same_head = lane_r == lane_c
    band = []

    def intra():
        g2 = g * LOG2E
        qb = q.astype(BF16)
        kxb = kx.astype(BF16)
        acc = jnp.zeros((tm, MIX), F32)
        for j in range(c):
            diff = jnp.where(pos >= j, g2 - row_of_chunk(g2, j), MASK_VALUE)
            w = _group_sum(qb * row_of_chunk(kxb, j) * jnp.exp2(diff).astype(BF16), e)
            yield
            acc = acc + w * row_of_chunk(v, j)
        band.append(acc)

    def inter():
        states = [s_ref[p] for p in range(N_PAIRS)]
        for ci in range(n):
            rows = slice(ci * c, (ci + 1) * c)
            for p in range(N_PAIRS):
                sl = slice(p * LANES, (p + 1) * LANES)
                upd = _dot_tn(vb_ref[rows, sl], kd_ref[rows, sl])
                os_ref[rows, sl] = _dot_nt(qd_ref[rows, sl], states[p].astype(BF16))
                dec = jnp.exp(ge_ref[ci * c:ci * c + 1, sl])
                states[p] = states[p] * dec + jnp.where(same_head, upd, 0.0)
            yield
        for p in range(N_PAIRS):
            s_ref[p] = states[p]

    yield
    yield from _lockstep_rounds([intra(), inter()])
    acc = band[0]

    o = acc + os_ref[...]
    ms = _group_sum(o * o, e) * (1.0 / HEAD)
    og = og_ref[blk, :].astype(F32)
    o_ref[blk, :] = (o * lax.rsqrt(ms + NORM_EPS) * gain_ref[...] * (og * _sigmoid(og))).astype(o_ref.dtype)
    yield


def _rwkv_stages(src, rk_ref, lnw_ref, lnb_ref, tri_ref, e_ref, o_ref, vfo_ref,
                 r_ref, k_ref, v_ref, kk_ref, a_ref, lw_ref, g_ref,
                 s_ref, kq_s, rq_s, bk_s, kx_s, nb_s, kd_s, dec_s, tkq_s, rqe_s, w_s, ol_s, oo_s):
    tm = r_ref.shape[0]
    c = RWKV_CHUNK
    n = tm // c
    first = pl.program_id(1) == 0
    has_vres = "vf" in src

    group = 2
    gr = group * c
    n_groups = n // group

    def proj(gi):
        g0 = gi * gr
        rows = slice(g0, g0 + gr)

        def token_shift(cur, ref, halo_ref):
            if gi == 0:
                prev = jnp.where(first, 0.0, halo_ref[...].astype(F32))
            else:
                prev = ref[g0 - halo_ref.shape[0]:g0, :].astype(F32)
            prev = prev[prev.shape[0] - 1:, :]
            row = lax.broadcasted_iota(jnp.int32, cur.shape, 0)
            return jnp.where(row == 0, prev, pltpu.roll(cur, 1, 0))

        p = src["p"][rows, :].astype(F32)
        pm = p + (token_shift(p, src["p"], src["ph"]) - p) * src["mu"][...]
        yield
        k_raw = pm[:, MIX:2 * MIX]
        v = pm[:, 2 * MIX:3 * MIX]
        lora = pm[:, 3 * MIX:3 * MIX + LANES]
        g_lo = pm[:, 3 * MIX + LANES:3 * MIX + 2 * LANES]
        lane = lax.broadcasted_iota(jnp.int32, (gr, LANES), 1)
        z = jnp.where(lane < HEAD, jnp.tanh(lora), lora)
        wa = jnp.dot(z.astype(BF16), src["wa2"][...], preferred_element_type=F32)
        g_ref[rows, :] = jnp.dot(_sigmoid(g_lo).astype(BF16), src["g2"][...],
                                 preferred_element_type=F32).astype(BF16)
        r_ref[rows, :] = pm[:, 0:MIX].astype(BF16)
        yield
        w_raw = _log_sigmoid(src["w0"][...] + wa[:, :MIX]) - 0.5
        lw_ref[rows, :] = -jnp.exp(w_raw)
        a = _sigmoid(src["a0"][...] + wa[:, MIX:])
        a_ref[rows, :] = a.astype(BF16)
        k_ref[rows, :] = (k_raw * (1.0 + (a - 1.0) * src["k_a"][...])).astype(BF16)
        yield
        if has_vres:
            m = src["misc"][rows, :]
            ms = m + (token_shift(m, src["misc"], src["misch"]) - m) * src["vmu"][...]
            mix = _sigmoid(src["v0"][...] + jnp.dot(ms.astype(BF16), src["v2"][...], preferred_element_type=F32))
            v = v + (src["vf"][rows, :].astype(F32) - v) * mix
        else:
            vfo_ref[rows, :] = v.astype(BF16)
        v_ref[rows, :] = v.astype(BF16)
        kk = k_raw * src["k_k"][...]
        ss = _group_sum(kk * kk, e_ref[...])
        yield
        kk_ref[rows, :] = (kk * lax.rsqrt(jnp.maximum(ss, 1e-24))).astype(BF16)

    def prep(gi):
        rows = slice(gi * gr, (gi + 1) * gr)
        lw = lw_ref[rows, :]
        gam = _exact_dot(tri_ref[0:gr, 0:gr], lw)
        yield
        g_last = jnp.broadcast_to(gam.reshape(group, c, MIX)[:, c - 1:c, :], (group, c, MIX)).reshape(gr, MIX)
        kk = kk_ref[rows, :].astype(F32)
        kq_s[rows, :] = (kk * jnp.exp(gam - lw)).astype(BF16)
        rq_s[rows, :] = (r_ref[rows, :].astype(F32) * jnp.exp(gam)).astype(BF16)
        yield
        b = a_ref[rows, :].astype(F32) * kk
        k = k_ref[rows, :].astype(F32)
        e_inv = jnp.exp(-gam)
        bk_s[rows, :] = (b * e_inv).astype(BF16)
        kx_s[rows, :] = (k * e_inv).astype(BF16)
        yield
        e_dec = jnp.exp(g_last - gam)
        nb_s[rows, :] = (-(b * e_dec)).astype(BF16)
        kd_s[rows, :] = (k * e_dec).astype(BF16)
        dec_s[rows, :] = jnp.exp(g_last)
        yield

    lane = lax.broadcasted_iota(jnp.int32, (c, LANES), 1)
    t_idx = lax.broadcasted_iota(jnp.int32, (c, LANES), 0)
    s_idx = lane % HEAD
    left = lane < HEAD
    strict = t_idx > s_idx
    incl = t_idx >= s_idx
    eye = jnp.where(t_idx == s_idx, 1.0, 0.0)
    lane_r = lax.broadcasted_iota(jnp.int32, (LANES, LANES), 0) // HEAD
    lane_c = lax.broadcasted_iota(jnp.int32, (LANES, LANES), 1) // HEAD
    same_head = lane_r == lane_c

    def halves(x):
        zero = jnp.zeros_like(x)
        return jnp.where(left, x, zero), jnp.where(left, zero, x)

    def bdiag(x):
        x0, x1 = halves(x.astype(BF16))
        return jnp.concatenate([x0, x1], axis=0)

    def mm(a, rhs):
        return jnp.dot(a.astype(BF16), rhs, preferred_element_type=F32)


    def local_chain(start, p):
        rows = slice(start, start + c)
        sl = slice(p * LANES, (p + 1) * LANES)
        kq, rq = kq_s[rows, sl], rq_s[rows, sl]
        lhs = jnp.concatenate([kq, rq], axis=0)
        rhs = jnp.concatenate(halves(bk_s[rows, sl]) + halves(kx_s[rows, sl]), axis=0)
        aa = _dot_nt(lhs, rhs)
        yield
        a_ab = jnp.where(strict, aa[:c, :LANES], 0.0)
        a_ak = jnp.where(strict, aa[:c, LANES:], 0.0)
        a_rb = jnp.where(incl, aa[c:, :LANES], 0.0)
        a_rk = jnp.where(incl, aa[c:, LANES:], 0.0)
        yo = mm(jnp.concatenate([a_ak, a_rk], axis=0), bdiag(v_ref[rows, sl]))
        yield
        pw = -a_ab
        tinv = eye + pw
        pw = mm(pw, bdiag(pw))
        yield
        for _ in range(4):
            res = mm(pw, jnp.concatenate([bdiag(tinv), bdiag(pw)], axis=1))
            yield
            tinv = tinv + res[:, :LANES]
            pw = res[:, LANES:]
        tinv = tinv + mm(pw, bdiag(tinv))
        yield
        tw = mm(tinv, jnp.concatenate([bdiag(kq), bdiag(yo[:c])], axis=1))
        yield
        ar = mm(a_rb, jnp.concatenate([bdiag(tw[:, :LANES]), bdiag(tw[:, LANES:])], axis=1))
        yield
        tkq_s[rows, sl] = tw[:, :LANES].astype(BF16)
        w_s[rows, sl] = tw[:, LANES:]
        rqe_s[rows, sl] = (rq.astype(F32) - ar[:, :LANES]).astype(BF16)
        ol_s[rows, sl] = yo[c:] - ar[:, LANES:]

    def seq_group(gi):
        for u in range(group):
            start = (gi * group + u) * c
            rows = slice(start, start + c)
            states = [s_ref[p] for p in range(N_PAIRS)]
            xs = []
            for p in range(N_PAIRS):
                sl = slice(p * LANES, (p + 1) * LANES)
                lhs = jnp.concatenate([tkq_s[rows, sl], rqe_s[rows, sl]], axis=0)
                xs.append(_dot_nt(lhs, states[p].astype(BF16)))
            yield
            upds = []
            for p in range(N_PAIRS):
                sl = slice(p * LANES, (p + 1) * LANES)
                uu = xs[p][:c] + w_s[rows, sl]
                oo_s[rows, sl] = xs[p][c:] + ol_s[rows, sl]
                wv = jnp.concatenate([uu.astype(BF16), v_ref[rows, sl]], axis=0)
                wk = jnp.concatenate([nb_s[rows, sl], kd_s[rows, sl]], axis=0)
                upds.append(_dot_tn(wv, wk))
            yield
            for p in range(N_PAIRS):
                sl = slice(p * LANES, (p + 1) * LANES)
                s_ref[p] = states[p] * dec_s[start:start + 1, sl] + jnp.where(same_head, upds[p], 0.0)

    def local_group(gi):
        return [local_chain((gi * group + u) * c, p) for u in range(group) for p in range(N_PAIRS)]

    def post(gi):
        rows = slice(gi * gr, (gi + 1) * gr)
        e = e_ref[...]
        o = oo_s[rows, :]
        mean = _group_sum2(o, e) * (1.0 / HEAD)
        yield
        d = o - mean
        var = _group_sum2(d * d, e) * (1.0 / HEAD)
        yield
        v = v_ref[rows, :].astype(F32)
        bonus = _group_sum2(r_ref[rows, :].astype(F32) * k_ref[rows, :].astype(F32) * rk_ref[...], e)
        yield
        out = d * lax.rsqrt(var + RWKV_LN_EPS) * lnw_ref[...] + lnb_ref[...] + bonus * v
        o_ref[rows, :] = (out * g_ref[rows, :].astype(F32)).astype(o_ref.dtype)

    for step in range(n_groups + 4):
        chains = []
        if step < n_groups:
            chains.append(proj(step))
        if 0 <= step - 1 < n_groups:
            chains.append(prep(step - 1))
        if 0 <= step - 2 < n_groups:
            chains += local_group(step - 2)
        if 0 <= step - 3 < n_groups:
            chains.append(seq_group(step - 3))
        if 0 <= step - 4 < n_groups:
            chains.append(post(step - 4))
        yield from _lockstep_rounds(chains)


HGRN_TILE = 256

_RWKV_SRC = ("p", "ph", "mu", "w0", "a0", "wa2", "g2", "k_k", "k_a")
_RWKV_SRC_VRES = ("misc", "misch", "vmu", "v0", "v2", "vf")


def _mixers_kernel(*refs, has_vres):
    names = _RWKV_SRC + (_RWKV_SRC_VRES if has_vres else ())
    refs = list(refs)
    hgrn_in, refs = refs[:8], refs[8:]
    src, refs = dict(zip(names, refs[:len(names)])), refs[len(names):]
    (rk_ref, lnw_ref, lnb_ref, tri64_ref), refs = refs[:4], refs[4:]
    if has_vres:
        (oh_ref, or_ref), refs = refs[:2], refs[2:]
        vfo_ref = None
    else:
        (oh_ref, or_ref, vfo_ref), refs = refs[:3], refs[3:]
    hgrn_scratch, rwkv_scratch = refs[:6], refs[6:]
    e_ref = hgrn_in[6]

    @pl.when(pl.program_id(1) == 0)
    def _():
        hgrn_scratch[0][...] = jnp.zeros_like(hgrn_scratch[0])
        rwkv_scratch[7][...] = jnp.zeros_like(rwkv_scratch[7])

    def hgrn_all():
        for row0 in range(0, oh_ref.shape[0], HGRN_TILE):
            yield from _hgrn_stages(*hgrn_in, oh_ref, *hgrn_scratch, row0)

    rwkv_all = _rwkv_stages(src, rk_ref, lnw_ref, lnb_ref, tri64_ref, e_ref, or_ref, vfo_ref, *rwkv_scratch)
    hgrn = hgrn_all()
    n_h, n_r, tick, done_h = 3, 5, 0, False
    for _ in rwkv_all:
        if not done_h and ((tick + 1) * n_h) // n_r > (tick * n_h) // n_r:
            done_h = next(hgrn, "done") == "done"
        tick += 1
    for _ in hgrn:
        pass


def _mixers(main, aux, lower, o_gain, prm, v_first, r_k, ln_w, ln_b, bsz, seq, tm):
    nt = seq // tm
    has_vres = v_first is not None
    ch = COL_HGRN // MIX
    pw = 4 * MIX
    cp = COL_RWKV // pw
    halo = 16
    row = lambda cidx: (lambda b, i: (b * nt + i, cidx))
    const = lambda b, i: (0, 0)

    def halo_map(rows, cidx):
        per = tm // rows
        return lambda b, i: (jnp.maximum((b * nt + i) * per - 1, 0), cidx)

    spec = pl.BlockSpec((tm, MIX), row(0))
    vec = pl.BlockSpec((1, MIX), const)
    in_specs = [pl.BlockSpec((tm, MIX), row(ch)), pl.BlockSpec((tm, MIX), row(ch + 1)),
                pl.BlockSpec((tm, MIX), row(ch + 2)), spec, vec, vec,
                pl.BlockSpec((MIX // 2, MIX // 2), const),
                pl.BlockSpec((HGRN_TILE, HGRN_TILE), const),
                pl.BlockSpec((tm, pw), row(cp)),
                pl.BlockSpec((halo, pw), halo_map(halo, cp)),
                pl.BlockSpec((1, pw), const), vec, vec,
                pl.BlockSpec((LANES, 2 * MIX), const),
                pl.BlockSpec((LANES, MIX), const), vec, vec]
    args = [main, main, main, aux, lower, o_gain, _block_ones(MIX // 2), _block_tril(HGRN_TILE, HGRN_SUB),
            main, main, prm["mu"], prm["w0"], prm["a0"], prm["wa2"], prm["g2"], prm["k_k"], prm["k_a"]]
    if has_vres:
        in_specs += [pl.BlockSpec((tm, LANES), row(MIX // LANES)),
                     pl.BlockSpec((8, LANES), halo_map(8, MIX // LANES)),
                     pl.BlockSpec((1, LANES), const), vec,
                     pl.BlockSpec((LANES, MIX), const), spec]
        args += [aux, aux, prm["vmu"], prm["v0"], prm["v2"], v_first]
    in_specs += [vec, vec, vec, pl.BlockSpec((tm, tm), const)]
    args += [r_k, ln_w, ln_b, _block_tril(tm, RWKV_CHUNK)]
    h16 = pltpu.VMEM((HGRN_TILE, MIX), BF16)
    h32 = pltpu.VMEM((HGRN_TILE, MIX), F32)
    s16 = pltpu.VMEM((tm, MIX), BF16)
    s32 = pltpu.VMEM((tm, MIX), F32)
    state = pltpu.VMEM((N_PAIRS, LANES, LANES), F32)
    out = jax.ShapeDtypeStruct((bsz * seq, MIX), BF16)
    n_out = 2 if has_vres else 3
    res = pl.pallas_call(
        functools.partial(_mixers_kernel, has_vres=has_vres),
        grid=(bsz, nt),
        in_specs=in_specs,
        out_specs=[spec] * n_out,
        out_shape=[out] * n_out,
        scratch_shapes=[state, h16, h16, h16, h32, h32,
                        s16, s16, s16, s16, s16, s32, s16,
                        state, s16, s16, s16, s16, s16, s16, s32, s16, s16, s32, s32, s32],
        compiler_params=_cparams(("parallel", "arbitrary")),
        name="hgrn_rwkv",
    )(*args)
    return (res[0], res[1], v_first) if has_vres else tuple(res)


def _merge_kernel(yf_ref, yh_ref, yr_ref, gf_ref, gh_ref, gr_ref, x_ref, gt_ref, wb_ref, wo_ref, o_ref):
    def branch(y_ref, g_ref, i):
        return _sigmoid(g_ref[...].astype(F32)) * jnp.dot(y_ref[...], wb_ref[i], preferred_element_type=F32)

    merged = branch(yf_ref, gf_ref, 0) + branch(yh_ref, gh_ref, 1) + branch(yr_ref, gr_ref, 2)
    o_ref[...] = x_ref[...] + gt_ref[0] * jnp.dot(merged.astype(BF16), wo_ref[...], preferred_element_type=F32)


def _merge(y_fox, y_hgrn, y_rwkv, main, x, gt, w_branch, w_out, seq, tm):
    t, d = x.shape
    tps = seq // tm
    yspec = pl.BlockSpec((tm, MIX), lambda i: (i, 0))
    gspec = lambda cidx: pl.BlockSpec((tm, d), lambda i: (i, COL_GATE // d + cidx))
    return pl.pallas_call(
        _merge_kernel,
        grid=(t // tm,),
        in_specs=[yspec, yspec, yspec, gspec(0), gspec(1), gspec(2),
                  pl.BlockSpec((tm, d), lambda i: (i, 0)),
                  pl.BlockSpec((1, 1, d), lambda i: (i // tps, 0, 0)),
                  pl.BlockSpec((3, MIX, d), lambda i: (0, 0, 0)),
                  pl.BlockSpec((d, d), lambda i: (0, 0))],
        out_specs=pl.BlockSpec((tm, d), lambda i: (i, 0)),
        out_shape=jax.ShapeDtypeStruct((t, d), F32),
        compiler_params=_cparams(("parallel",)),
        name="merge_out",
    )(y_fox, y_hgrn, y_rwkv, main, main, main, x, gt, w_branch, w_out)


FFN_HALO = 16
FFN_TF = 256


def _ffn_body(x_ref, xh_ref, sc_ref, sh_ref, gt_ref, wup_ref, cw_ref, cb_ref, wd_ref, o_ref, h_ref, act_ref,
              *, tps):
    def normed(x):
        ms = jnp.mean(x * x, axis=-1, keepdims=True)
        return x * lax.rsqrt(ms + NORM_EPS) * sc_ref[0] + sh_ref[0]

    seq_start = pl.program_id(0) % tps == 0
    h_ref[0:FFN_HALO, :] = jnp.where(seq_start, 0.0, normed(xh_ref[...])).astype(BF16)
    h_ref[FFN_HALO:, :] = normed(x_ref[...]).astype(BF16)

    def up_proj(col):
        return jnp.dot(h_ref[...], wup_ref[:, col:col + FFN_TF], preferred_element_type=F32)

    def conv(u, col):
        cw = cw_ref[:, col:col + FFN_TF]
        y = cb_ref[:, col:col + FFN_TF] + cw[0:1] * pltpu.roll(u, 2, 0) + cw[1:2] * pltpu.roll(u, 1, 0) + cw[2:3] * u
        return y[FFN_HALO:]

    cols = [base + j * FFN_TF for j in range(D_FF // FFN_TF) for base in (0, D_FF)]
    u_next = up_proj(cols[0])
    val = None
    for i, col in enumerate(cols):
        u = u_next
        if i + 1 < len(cols):
            u_next = up_proj(cols[i + 1])
        y = conv(u, col)
        if i % 2 == 0:
            val = y
        else:
            act_ref[:, col - D_FF:col - D_FF + FFN_TF] = (y * _sigmoid(y) * val).astype(BF16)

    down = jnp.dot(act_ref[...], wd_ref[...], preferred_element_type=F32)
    o_ref[...] = x_ref[...] + gt_ref[0] * down


def _ffn(x, scale, shift, gt, w_up, conv_w, conv_b, w_down, seq, tm):
    t, d = x.shape
    tps = seq // tm
    per = tm // FFN_HALO
    bmap = lambda i: (i // tps, 0, 0)
    const = lambda i: (0, 0)
    resident = dict(pipeline_mode=pl.Buffered(1))
    return pl.pallas_call(
        functools.partial(_ffn_body, tps=tps),
        grid=(t // tm,),
        in_specs=[pl.BlockSpec((tm, d), lambda i: (i, 0)),
                  pl.BlockSpec((FFN_HALO, d), lambda i: (jnp.maximum(i * per - 1, 0), 0)),
                  pl.BlockSpec((1, 1, d), bmap),
                  pl.BlockSpec((1, 1, d), bmap),
                  pl.BlockSpec((1, 1, d), bmap),
                  pl.BlockSpec((d, 2 * D_FF), const, **resident),
                  pl.BlockSpec((3, 2 * D_FF), const),
                  pl.BlockSpec((1, 2 * D_FF), const),
                  pl.BlockSpec((D_FF, d), const, **resident)],
        out_specs=pl.BlockSpec((tm, d), lambda i: (i, 0)),
        out_shape=jax.ShapeDtypeStruct((t, d), F32),
        scratch_shapes=[pltpu.VMEM((tm + FFN_HALO, d), BF16), pltpu.VMEM((tm, D_FF), BF16)],
        compiler_params=_cparams(("parallel",)),
        name="conv_mlp",
    )(x, x, scale, shift, gt, w_up, conv_w, conv_b, w_down)


def _pack_in_proj(w_in, vres_down):
    depth, d, _ = w_in.shape
    fox, hgrn, rwkv, gate = 0, 1544, 3592, 5384
    w_in = w_in.astype(BF16)
    vres_down = vres_down.astype(BF16)
    z = lambda n: jnp.zeros((depth, d, n), w_in.dtype)
    main = jnp.concatenate([
        w_in[..., gate:gate + 3072],
        w_in[..., fox:fox + 1536],
        w_in[..., hgrn:hgrn + 512], w_in[..., hgrn + 1024:hgrn + 2048],
        w_in[..., rwkv:rwkv + 1792],
        z(N_MAIN - 7936)], axis=-1)
    vres = jnp.concatenate([jnp.zeros((1, d, VRES_RANK), w_in.dtype), vres_down], axis=0)
    aux = jnp.concatenate([
        w_in[..., hgrn + 512:hgrn + 1024],
        w_in[..., fox + 1536:fox + 1544], z(AUX_VRES_LANE - 8),
        vres, z(LANES - AUX_VRES_LANE - VRES_RANK)], axis=-1)
    assert main.shape[-1] == N_MAIN and aux.shape[-1] == N_AUX
    return main, aux


def _tile_heads(v):
    return jnp.tile(v, N_HEADS).reshape(1, MIX)


def _forward(x, c, w_ada, b_ada, norm1_g, norm2_g, w_in, fox_b_f, fox_q_gain, fox_k_gain,
             hgrn_lb, hgrn_o_gain, rwkv_mu, rwkv_w0, rwkv_w2, rwkv_a0, rwkv_a2, rwkv_g2,
             rwkv_k_k, rwkv_k_a, rwkv_r_k, rwkv_ln_w, rwkv_ln_b, rwkv_vres_down, rwkv_vres_mu,
             rwkv_v0, rwkv_v2, w_branch, w_out, w_up, conv_w, conv_b, w_down, *, tiles):
    bsz, seq, d = x.shape
    depth = w_in.shape[0]
    t = bsz * seq

    lb_prob = jax.nn.softmax(hgrn_lb.astype(F32), axis=0)
    hgrn_lower = jnp.cumsum(lb_prob, axis=0) - lb_prob[0]

    mod = _ada_mod(c, w_ada, b_ada)
    w_main, w_aux = _pack_in_proj(w_in, rwkv_vres_down)
    w_branch16, w_out16 = w_branch.astype(BF16), w_out.astype(BF16)
    w_up16, w_down16 = w_up.astype(BF16), w_down.astype(BF16)

    xf = x.reshape(t, d)
    v_first = None
    for l in range(depth):
        sh1, sc1, gt1, sh2, sc2, gt2 = [m.reshape(bsz, 1, d) for m in jnp.split(mod[l], 6, axis=-1)]
        scale1 = norm1_g[l] * (1.0 + sc1)
        b_f = jnp.zeros((1, LANES), F32).at[0, AUX_FF_LANE:AUX_FF_LANE + N_HEADS].set(fox_b_f[l])
        main, aux, qh, kh, vh = _in_proj(xf, scale1, sh1, w_main[l], w_aux[l],
                                         _tile_heads(fox_q_gain[l]) * HEAD ** -0.5, _tile_heads(fox_k_gain[l]), b_f,
                                         bsz, seq, tiles["proj_tm"], tiles["proj_tn"])
        y_fox = _fox_attn(qh, kh, vh, tiles["fox_tq"], tiles["fox_sub"], tiles["fox_tk"])

        wa2 = jnp.zeros((LANES, 2 * MIX), F32)
        wa2 = wa2.at[:HEAD, :MIX].set(rwkv_w2[l]).at[HEAD:, MIX:].set(rwkv_a2[l])
        prm = {
            "mu": jnp.concatenate([rwkv_mu[l], jnp.zeros((4 * MIX - rwkv_mu.shape[1],), F32)]).reshape(1, 4 * MIX),
            "w0": rwkv_w0[l].reshape(1, MIX), "a0": rwkv_a0[l].reshape(1, MIX),
            "wa2": wa2.astype(BF16), "g2": rwkv_g2[l].astype(BF16),
            "k_k": rwkv_k_k[l].reshape(1, MIX), "k_a": rwkv_k_a[l].reshape(1, MIX),
        }
        if l > 0:
            vmu = jnp.zeros((1, LANES), F32).at[0, AUX_VRES_LANE:AUX_VRES_LANE + VRES_RANK].set(rwkv_vres_mu[l - 1])
            v2 = jnp.zeros((LANES, MIX), F32).at[AUX_VRES_LANE:AUX_VRES_LANE + VRES_RANK].set(rwkv_v2[l - 1])
            prm.update(vmu=vmu, v0=rwkv_v0[l - 1].reshape(1, MIX), v2=v2.astype(BF16))
        y_hgrn, y_rwkv, v_first = _mixers(main, aux, hgrn_lower[l].reshape(1, MIX), _tile_heads(hgrn_o_gain[l]),
                                          prm, v_first, rwkv_r_k[l].reshape(1, MIX), rwkv_ln_w[l].reshape(1, MIX),
                                          rwkv_ln_b[l].reshape(1, MIX), bsz, seq, tiles["rwkv_tm"])

        xf = _merge(y_fox, y_hgrn, y_rwkv, main, xf, gt1, w_branch16[l], w_out16[l], seq, tiles["merge_tm"])

        scale2 = norm2_g[l] * (1.0 + sc2)
        xf = _ffn(xf, scale2, sh2, gt2, w_up16[l], conv_w[l], conv_b[l].reshape(1, -1), w_down16[l], seq,
                  tiles["ffn_tm"])
    return xf.reshape(bsz, seq, d)


def _tiles_for(seq):
    cap = lambda n: min(n, seq)
    return dict(proj_tm=cap(1024), proj_tn=1024, fox_tq=cap(1024), fox_sub=256, fox_tk=cap(1024),
                rwkv_tm=cap(512), merge_tm=cap(1024), ffn_tm=cap(1024))


def kernel(x, c, w_ada, b_ada, norm1_g, norm2_g, w_in, fox_b_f, fox_q_gain, fox_k_gain, hgrn_lb, hgrn_o_gain, rwkv_mu, rwkv_w0, rwkv_w2, rwkv_a0, rwkv_a2, rwkv_g2, rwkv_k_k, rwkv_k_a, rwkv_r_k, rwkv_ln_w, rwkv_ln_b, rwkv_vres_down, rwkv_vres_mu, rwkv_v0, rwkv_v2, w_branch, w_out, w_up, conv_w, conv_b, w_down):
    return _forward(x, c, w_ada, b_ada, norm1_g, norm2_g, w_in, fox_b_f, fox_q_gain, fox_k_gain,
                    hgrn_lb, hgrn_o_gain, rwkv_mu, rwkv_w0, rwkv_w2, rwkv_a0, rwkv_a2, rwkv_g2,
                    rwkv_k_k, rwkv_k_a, rwkv_r_k, rwkv_ln_w, rwkv_ln_b, rwkv_vres_down, rwkv_vres_mu,
                    rwkv_v0, rwkv_v2, w_branch, w_out, w_up, conv_w, conv_b, w_down,
                    tiles=_tiles_for(x.shape[1]))
```

```python
import functools

import jax
import jax.numpy as jnp
import numpy as np
from jax import lax
from jax.experimental import pallas as pl
from jax.experimental.pallas import tpu as pltpu

F32 = jnp.float32
BF16 = jnp.bfloat16

MIX = 512
HEAD = 64
N_HEADS = MIX // HEAD
N_PAIRS = N_HEADS // 2
LANES = 128
D_FF = 2816
NORM_EPS = 1e-6
RWKV_LN_EPS = 64e-5
MASK_VALUE = -1e30
LOG_FLOOR = 1e-30
LOG2E = 1.4426950408889634
VRES_RANK = 32

HGRN_SUB = 16
RWKV_CHUNK = 64

COL_GATE = 0
COL_FOX = 3072
COL_HGRN = 4608
COL_RWKV = 6144
N_MAIN = 8192
N_AUX = 640
AUX_FF_LANE = 0
AUX_VRES_LANE = 32

VMEM_LIMIT = 56 * 1024 * 1024


def _cparams(sem):
    return pltpu.CompilerParams(dimension_semantics=sem, vmem_limit_bytes=VMEM_LIMIT)


def _split3(x):
    hi = x.astype(BF16)
    r1 = x - hi.astype(F32)
    mid = r1.astype(BF16)
    lo = (r1 - mid.astype(F32)).astype(BF16)
    return hi, mid, lo


def _exact_dot(a01, x):
    hi, mid, lo = _split3(x)
    d = lambda b: jnp.dot(a01, b, preferred_element_type=F32)
    return d(hi) + d(mid) + d(lo)


def _group_sum(x, e):
    w = e.shape[0]
    xb = x.astype(BF16)
    if x.shape[1] == w:
        return jnp.dot(xb, e, preferred_element_type=F32)
    parts = [jnp.dot(xb[:, i:i + w], e, preferred_element_type=F32) for i in range(0, x.shape[1], w)]
    return jnp.concatenate(parts, axis=1)


def _group_sum2(x, e):
    hi = x.astype(BF16)
    lo = x - hi.astype(F32)
    return _group_sum(hi, e) + _group_sum(lo, e)


def _sigmoid(x):
    return 1.0 / (1.0 + jnp.exp(-x))


def _log_sigmoid(x):
    return jnp.minimum(x, 0.0) - jnp.log1p(jnp.exp(-jnp.abs(x)))


def _dot_nt(a, b):
    return lax.dot_general(a, b, (((1,), (1,)), ((), ())), preferred_element_type=F32)


def _dot_tn(a, b):
    return lax.dot_general(a, b, (((0,), (0,)), ((), ())), preferred_element_type=F32)


def _run_lockstep(gens):
    live = list(gens)
    while live:
        nxt = []
        for g in live:
            try:
                next(g)
                nxt.append(g)
            except StopIteration:
                pass
        live = nxt


def _lockstep_rounds(gens):
    live = list(gens)
    while live:
        nxt = []
        for g in live:
            try:
                next(g)
                nxt.append(g)
            except StopIteration:
                pass
        live = nxt
        yield


def _block_ones(width):
    g = np.arange(width) // HEAD
    return jnp.asarray(g[:, None] == g[None, :], dtype=BF16)


def _block_tril(n, blk):
    i = np.arange(n)
    m = (i[:, None] >= i[None, :]) & ((i[:, None] // blk) == (i[None, :] // blk))
    return jnp.asarray(m, dtype=BF16)


def _ada_kernel(c_ref, w_ref, b_ref, o_ref):
    c = c_ref[...]
    cond = c * _sigmoid(c)
    o_ref[0] = jnp.dot(cond.astype(BF16), w_ref[0].astype(BF16), preferred_element_type=F32) + b_ref[0]


def _ada_mod(c, w_ada, b_ada, tn=1536):
    depth, d, n = w_ada.shape
    bsz = c.shape[0]
    return pl.pallas_call(
        _ada_kernel,
        grid=(depth, n // tn),
        in_specs=[pl.BlockSpec((bsz, d), lambda l, j: (0, 0)),
                  pl.BlockSpec((1, d, tn), lambda l, j: (l, 0, j)),
                  pl.BlockSpec((1, 1, tn), lambda l, j: (l, 0, j))],
        out_specs=pl.BlockSpec((1, bsz, tn), lambda l, j: (l, 0, j)),
        out_shape=jax.ShapeDtypeStruct((depth, bsz, n), F32),
        compiler_params=_cparams(("parallel", "parallel")),
        name="ada_mod",
    )(c, w_ada, b_ada.reshape(depth, 1, n))


FOX_PREP_ROWS = 512


def _fox_prep_stages(qkv_s, misc_s, qg_ref, kg_ref, bf_ref, e_ref, tri_ref, sel_ref,
                     qo_ref, ko_ref, vo_ref, carry_ref, row0):
    tm = FOX_PREP_ROWS
    blk = slice(row0, row0 + tm)
    e = e_ref[...]

    def qk_norm(cols, gain):
        xx = qkv_s[blk, cols].astype(F32)
        ss = _group_sum(xx * xx, e)
        return xx * lax.rsqrt(ss * (1.0 / HEAD) + NORM_EPS) * gain

    qn = qk_norm(slice(0, MIX), qg_ref[...])
    yield
    kn = qk_norm(slice(MIX, 2 * MIX), kg_ref[...])
    vv = qkv_s[blk, 2 * MIX:3 * MIX].astype(F32)
    yield

    lf = _log_sigmoid(misc_s[blk, :] + bf_ref[...])
    cum = _exact_dot(tri_ref[...], lf) + carry_ref[...]
    carry_ref[...] = cum[tm - 1:tm, :]
    yield
    hi, mid, lo = [s.astype(F32) for s in _split3(cum)]
    lane = lax.broadcasted_iota(jnp.int32, (tm, LANES), 1)
    fcat = jnp.where(lane < 8, hi, jnp.where(lane < 16, pltpu.roll(mid, 8, 1),
                     jnp.where(lane < 24, pltpu.roll(lo, 16, 1), jnp.where(lane == 24, 1.0, 0.0))))
    ext = jnp.dot(fcat.astype(BF16), sel_ref[...], preferred_element_type=F32)
    yield

    for p in range(N_PAIRS):
        sl = slice(p * LANES, (p + 1) * LANES)
        qp, kp, vp = qn[:, sl], kn[:, sl], vv[:, sl]
        for par in range(2):
            h = 2 * p + par
            base = HEAD if par == 0 else 0
            keep = (lane < HEAD) if par == 0 else (lane >= HEAD)
            qo_ref[0, h, :, blk] = jnp.where(keep, qp, ext[:, 2 * h * LANES:(2 * h + 1) * LANES]).T.astype(BF16)
            ko_ref[0, h, blk, :] = jnp.where(keep, kp, ext[:, (2 * h + 1) * LANES:(2 * h + 2) * LANES]).astype(BF16)
            vo_ref[0, h, :, blk] = jnp.where(keep, vp, jnp.where(lane == base, 1.0, 0.0)).T.astype(BF16)
        yield


def _fox_sel():
    sel = np.zeros((LANES, N_HEADS, 2, LANES), np.float32)
    for h in range(N_HEADS):
        base = HEAD if h % 2 == 0 else 0
        for i in range(3):
            sel[8 * i + h, h, 0, base + i] = 1.0
            sel[24, h, 0, base + 3 + i] = 1.0
            sel[24, h, 1, base + i] = 1.0
            sel[8 * i + h, h, 1, base + 3 + i] = -1.0
    return jnp.asarray(sel.reshape(LANES, N_HEADS * 2 * LANES), dtype=BF16)


def _in_proj_kernel(x_ref, sc_ref, sh_ref, w_ref, wa_ref, qg_ref, kg_ref, bf_ref, e_ref, tri_ref, sel_ref,
                    o_ref, oa_ref, qo_ref, ko_ref, vo_ref, h_ref, qkv_s, misc_s, carry_ref, *, n_main, tps):
    j = pl.program_id(1)
    tm, tn = o_ref.shape
    fox_blk = COL_FOX // tn + 1
    prep_rows = list(range(0, tm, FOX_PREP_ROWS))
    prep_steps = [fox_blk + 2 + u for u in range(len(prep_rows))]

    @pl.when(j == 0)
    def _():
        x = x_ref[...]
        ms = jnp.mean(x * x, axis=-1, keepdims=True)
        h_ref[...] = (x * lax.rsqrt(ms + NORM_EPS) * sc_ref[0] + sh_ref[0]).astype(BF16)
        aux = jnp.dot(h_ref[...], wa_ref[...], preferred_element_type=F32)
        oa_ref[...] = aux
        misc_s[...] = aux[:, MIX:MIX + LANES]

        @pl.when(pl.program_id(0) % tps == 0)
        def _():
            carry_ref[...] = jnp.zeros_like(carry_ref)

    def block():
        return jnp.dot(h_ref[...], w_ref[...], preferred_element_type=F32).astype(BF16)

    plain = (j >= 1) & (j != fox_blk) & (j != fox_blk + 1)
    for st in prep_steps:
        plain = plain & (j != st)

    @pl.when(plain)
    def _():
        o_ref[...] = block()

    @pl.when(j == fox_blk)
    def _():
        o = block()
        o_ref[...] = o
        qkv_s[:, 0:2 * MIX] = o[:, 0:2 * MIX]

    @pl.when(j == fox_blk + 1)
    def _():
        o = block()
        o_ref[...] = o
        qkv_s[:, 2 * MIX:3 * MIX] = o[:, 0:MIX]

    for st, row0 in zip(prep_steps, prep_rows):
        @pl.when(j == st)
        def _(row0=row0):
            def matmul():
                yield
                sub = tm // 4
                for r in range(0, tm, sub):
                    o_ref[r:r + sub, :] = jnp.dot(h_ref[r:r + sub, :], w_ref[...],
                                                  preferred_element_type=F32).astype(BF16)
                    yield

            _run_lockstep([matmul(), _fox_prep_stages(qkv_s, misc_s, qg_ref, kg_ref, bf_ref, e_ref, tri_ref,
                                                       sel_ref, qo_ref, ko_ref, vo_ref, carry_ref, row0)])


def _in_proj(x, scale, shift, w, w_aux, q_gain, k_gain, b_f, bsz, seq, tm, tn):
    t, d = x.shape
    n = w.shape[1]
    n_aux = w_aux.shape[1]
    n_main = n // tn
    tps = seq // tm
    assert COL_FOX % tn == 0 and 2 * MIX == tn and tm % FOX_PREP_ROWS == 0
    assert COL_FOX // tn + 3 + tm // FOX_PREP_ROWS <= n_main + 1
    col = lambda i, j: (0, jnp.maximum(j - 1, 0))
    const = lambda i, j: (0, 0)
    head = lambda i, j: (i // tps, 0, i % tps, 0)
    head_t = lambda i, j: (i // tps, 0, 0, i % tps)
    out = jax.ShapeDtypeStruct((bsz, N_HEADS, seq, LANES), BF16)
    out_t = jax.ShapeDtypeStruct((bsz, N_HEADS, LANES, seq), BF16)
    return pl.pallas_call(
        functools.partial(_in_proj_kernel, n_main=n_main, tps=tps),
        grid=(t // tm, n_main + 1),
        in_specs=[pl.BlockSpec((tm, d), lambda i, j: (i, 0)),
                  pl.BlockSpec((1, 1, d), lambda i, j: (i // tps, 0, 0)),
                  pl.BlockSpec((1, 1, d), lambda i, j: (i // tps, 0, 0)),
                  pl.BlockSpec((d, tn), col),
                  pl.BlockSpec((d, n_aux), const),
                  pl.BlockSpec((1, MIX), const),
                  pl.BlockSpec((1, MIX), const),
                  pl.BlockSpec((1, LANES), const),
                  pl.BlockSpec((MIX // 2, MIX // 2), const),
                  pl.BlockSpec((FOX_PREP_ROWS, FOX_PREP_ROWS), const),
                  pl.BlockSpec((LANES, N_HEADS * 2 * LANES), const)],
        out_specs=[pl.BlockSpec((tm, tn), lambda i, j: (i, jnp.maximum(j - 1, 0))),
                   pl.BlockSpec((tm, n_aux), lambda i, j: (i, 0)),
                   pl.BlockSpec((1, N_HEADS, LANES, tm), head_t),
                   pl.BlockSpec((1, N_HEADS, tm, LANES), head),
                   pl.BlockSpec((1, N_HEADS, LANES, tm), head_t)],
        out_shape=[jax.ShapeDtypeStruct((t, n), BF16), jax.ShapeDtypeStruct((t, n_aux), F32), out_t, out, out_t],
        scratch_shapes=[pltpu.VMEM((tm, d), BF16), pltpu.VMEM((tm, 3 * MIX), BF16), pltpu.VMEM((tm, LANES), F32),
                        pltpu.VMEM((1, LANES), F32)],
        compiler_params=_cparams(("arbitrary", "arbitrary")),
        name="in_proj",
    )(x, scale, shift, w, w_aux, q_gain, k_gain, b_f, _block_ones(MIX // 2),
      _block_tril(FOX_PREP_ROWS, FOX_PREP_ROWS), _fox_sel())


def _fox_attn_kernel(qt_ref, k_ref, vt_ref, o_ref, *, tq, sub, tk):
    qi = pl.program_id(2)
    n_sub = tq // sub
    kpt = tq // tk
    chains = [(hh, r) for hh in range(2) for r in range(n_sub)]

    def step(j, carry, key_off):
        new = list(carry)
        start = pl.multiple_of(j * tk, tk)

        def chain(idx, hh, r):
            q_lo = r * sub
            nk = tk if key_off is None else min(q_lo + sub - key_off, tk)
            if nk <= 0:
                return
            if idx % 2 == 1:
                yield
            m, acc = carry[idx]
            qt = qt_ref[0, hh, :, q_lo:q_lo + sub]
            kb = k_ref[0, hh, pl.ds(start, nk), :]
            vtb = vt_ref[0, hh, :, pl.ds(start, nk)]
            st = jnp.dot(kb, qt, preferred_element_type=F32)
            yield
            if key_off is not None and key_off + nk > q_lo:
                kpos = lax.broadcasted_iota(jnp.int32, (nk, sub), 0) + key_off
                qpos = lax.broadcasted_iota(jnp.int32, (nk, sub), 1) + q_lo
                st = jnp.where(kpos <= qpos, st, MASK_VALUE)
            m_new = jnp.maximum(m, jnp.max(st, axis=0, keepdims=True))
            p = jnp.exp(st - m_new).astype(BF16)
            yield
            pv = jnp.dot(vtb, p, preferred_element_type=F32)
            yield
            new[idx] = (m_new, jnp.exp(m - m_new) * acc + pv)

        _run_lockstep([chain(i, hh, r) for i, (hh, r) in enumerate(chains)])
        return tuple(new)

    init = tuple((jnp.full((1, sub), MASK_VALUE, F32), jnp.zeros((LANES, sub), F32)) for _ in chains)
    carry = lax.fori_loop(0, qi * kpt, lambda j, c: step(j, c, None), init)
    for jd in range(kpt):
        carry = step(qi * kpt + jd, carry, jd * tk)
    row = lax.broadcasted_iota(jnp.int32, (LANES, tq), 0)
    accs = [jnp.concatenate([carry[hh * n_sub + r][1] for r in range(n_sub)], axis=1) for hh in range(2)]
    ot = jnp.where(row < HEAD, accs[0] / accs[0][HEAD:HEAD + 1, :], accs[1] / accs[1][0:1, :])
    o_ref[...] = ot.T.astype(o_ref.dtype)


def _fox_attn(qt, kh, vt, tq, sub, tk):
    bsz, _, seq, _ = kh.shape
    nq = seq // tq
    return pl.pallas_call(
        functools.partial(_fox_attn_kernel, tq=tq, sub=sub, tk=tk),
        grid=(bsz, N_PAIRS, nq),
        in_specs=[pl.BlockSpec((1, 2, LANES, tq), lambda b, p, i: (b, p, 0, i)),
                  pl.BlockSpec((1, 2, seq, LANES), lambda b, p, i: (b, p, 0, 0)),
                  pl.BlockSpec((1, 2, LANES, seq), lambda b, p, i: (b, p, 0, 0))],
        out_specs=pl.BlockSpec((tq, LANES), lambda b, p, i: (b * nq + i, p)),
        out_shape=jax.ShapeDtypeStruct((bsz * seq, MIX), BF16),
        compiler_params=_cparams(("parallel", "parallel", "arbitrary")),
        name="fox_attn",
    )(qt, kh, vt)


def _hgrn_stages(q_ref, i_ref, og_ref, f_ref, lb_ref, gain_ref, e_ref, tri_ref, o_ref,
                 s_ref, qd_ref, kd_ref, vb_ref, ge_ref, os_ref, row0):
    tm = qd_ref.shape[0]
    c = HGRN_SUB
    n = tm // c
    blk = slice(row0, row0 + tm)

    lb = lb_ref[...]
    gate = lb + (1.0 - lb) * _sigmoid(f_ref[blk, :])
    ell = jnp.log(jnp.maximum(gate, LOG_FLOOR))
    kx = 1.0 - gate
    qraw = q_ref[blk, :].astype(F32)
    q = qraw * _sigmoid(qraw)
    v = i_ref[blk, :].astype(F32)
    g = _exact_dot(tri_ref[...], ell)

    def row_of_chunk(a, j):
        a3 = a.reshape(n, c, MIX)
        return jnp.broadcast_to(a3[:, j:j + 1, :], (n, c, MIX)).reshape(tm, MIX)

    g_end = row_of_chunk(g, c - 1)
    qd_ref[...] = (q * jnp.exp(g)).astype(BF16)
    kd_ref[...] = (kx * jnp.exp(g_end - g)).astype(BF16)
    vb_ref[...] = v.astype(BF16)
    ge_ref[...] = g_end

    e = e_ref[...]
    pos = lax.broadcasted_iota(jnp.int32, (tm, MIX), 0) % c
    lane_r = lax.broadcasted_iota(jnp.int32, (LANES, LANES), 0) // HEAD
    lane_c = lax.broadcasted_iota(jnp.int32, (LANES, LANES), 1) // HEAD
    same_head = lane_r == lane_c
    band = []

    def intra():
        g2 = g * LOG2E
        qb = q.astype(BF16)
        kxb = kx.astype(BF16)
        acc = jnp.zeros((tm, MIX), F32)
        for j in range(c):
            diff = jnp.where(pos >= j, g2 - row_of_chunk(g2, j), MASK_VALUE)
            w = _group_sum(qb * row_of_chunk(kxb, j) * jnp.exp2(diff).astype(BF16), e)
            yield
            acc = acc + w * row_of_chunk(v, j)
        band.append(acc)

    def inter():
        yield
        states = [s_ref[p] for p in range(N_PAIRS)]
        for ci in range(n):
            rows = slice(ci * c, (ci + 1) * c)
            for p in range(N_PAIRS):
                sl = slice(p * LANES, (p + 1) * LANES)
                upd = _dot_tn(vb_ref[rows, sl], kd_ref[rows, sl])
                os_ref[rows, sl] = _dot_nt(qd_ref[rows, sl], states[p].astype(BF16))
                dec = jnp.exp(ge_ref[ci * c:ci * c + 1, sl])
                states[p] = states[p] * dec + jnp.where(same_head, upd, 0.0)
            yield
        for p in range(N_PAIRS):
            s_ref[p] = states[p]

    yield
    yield from _lockstep_rounds([intra(), inter()])
    acc = band[0]

    o = acc + os_ref[...]
    ms = _group_sum(o * o, e) * (1.0 / HEAD)
    og = og_ref[blk, :].astype(F32)
    o_ref[blk, :] = (o * lax.rsqrt(ms + NORM_EPS) * gain_ref[...] * (og * _sigmoid(og))).astype(o_ref.dtype)
    yield


def _rwkv_stages(src, rk_ref, lnw_ref, lnb_ref, tri_ref, e_ref, o_ref, vfo_ref,
                 r_ref, k_ref, v_ref, kk_ref, a_ref, lw_ref, g_ref,
                 s_ref, kq_s, rq_s, bk_s, kx_s, nb_s, kd_s, dec_s, tkq_s, rqe_s, w_s, ol_s, oo_s):
    tm = r_ref.shape[0]
    c = RWKV_CHUNK
    n = tm // c
    first = pl.program_id(1) == 0
    has_vres = "vf" in src

    group = 2
    gr = group * c
    n_groups = n // group

    def proj(gi):
        g0 = gi * gr
        rows = slice(g0, g0 + gr)

        def token_shift(cur, ref, halo_ref):
            if gi == 0:
                prev = jnp.where(first, 0.0, halo_ref[...].astype(F32))
            else:
                prev = ref[g0 - halo_ref.shape[0]:g0, :].astype(F32)
            prev = prev[prev.shape[0] - 1:, :]
            row = lax.broadcasted_iota(jnp.int32, cur.shape, 0)
            return jnp.where(row == 0, prev, pltpu.roll(cur, 1, 0))

        p = src["p"][rows, :].astype(F32)
        pm = p + (token_shift(p, src["p"], src["ph"]) - p) * src["mu"][...]
        yield
        k_raw = pm[:, MIX:2 * MIX]
        v = pm[:, 2 * MIX:3 * MIX]
        lora = pm[:, 3 * MIX:3 * MIX + LANES]
        g_lo = pm[:, 3 * MIX + LANES:3 * MIX + 2 * LANES]
        lane = lax.broadcasted_iota(jnp.int32, (gr, LANES), 1)
        z = jnp.where(lane < HEAD, jnp.tanh(lora), lora)
        wa = jnp.dot(z.astype(BF16), src["wa2"][...], preferred_element_type=F32)
        g_ref[rows, :] = jnp.dot(_sigmoid(g_lo).astype(BF16), src["g2"][...],
                                 preferred_element_type=F32).astype(BF16)
        r_ref[rows, :] = pm[:, 0:MIX].astype(BF16)
        yield
        w_raw = _log_sigmoid(src["w0"][...] + wa[:, :MIX]) - 0.5
        lw_ref[rows, :] = -jnp.exp(w_raw)
        a = _sigmoid(src["a0"][...] + wa[:, MIX:])
        a_ref[rows, :] = a.astype(BF16)
        k_ref[rows, :] = (k_raw * (1.0 + (a - 1.0) * src["k_a"][...])).astype(BF16)
        yield
        if has_vres:
            m = src["misc"][rows, :]
            ms = m + (token_shift(m, src["misc"], src["misch"]) - m) * src["vmu"][...]
            mix = _sigmoid(src["v0"][...] + jnp.dot(ms.astype(BF16), src["v2"][...], preferred_element_type=F32))
            v = v + (src["vf"][rows, :].astype(F32) - v) * mix
        else:
            vfo_ref[rows, :] = v.astype(BF16)
        v_ref[rows, :] = v.astype(BF16)
        kk = k_raw * src["k_k"][...]
        ss = _group_sum(kk * kk, e_ref[...])
        yield
        kk_ref[rows, :] = (kk * lax.rsqrt(jnp.maximum(ss, 1e-24))).astype(BF16)

    def prep(gi):
        rows = slice(gi * gr, (gi + 1) * gr)
        lw = lw_ref[rows, :]
        gam = _exact_dot(tri_ref[0:gr, 0:gr], lw)
        yield
        g_last = jnp.broadcast_to(gam.reshape(group, c, MIX)[:, c - 1:c, :], (group, c, MIX)).reshape(gr, MIX)
        kk = kk_ref[rows, :].astype(F32)
        kq_s[rows, :] = (kk * jnp.exp(gam - lw)).astype(BF16)
        rq_s[rows, :] = (r_ref[rows, :].astype(F32) * jnp.exp(gam)).astype(BF16)
        yield
        b = a_ref[rows, :].astype(F32) * kk
        k = k_ref[rows, :].astype(F32)
        e_inv = jnp.exp(-gam)
        bk_s[rows, :] = (b * e_inv).astype(BF16)
        kx_s[rows, :] = (k * e_inv).astype(BF16)
        yield
        e_dec = jnp.exp(g_last - gam)
        nb_s[rows, :] = (-(b * e_dec)).astype(BF16)
        kd_s[rows, :] = (k * e_dec).astype(BF16)
        dec_s[rows, :] = jnp.exp(g_last)
        yield

    lane = lax.broadcasted_iota(jnp.int32, (c, LANES), 1)
    t_idx = lax.broadcasted_iota(jnp.int32, (c, LANES), 0)
    s_idx = lane % HEAD
    left = lane < HEAD
    strict = t_idx > s_idx
    incl = t_idx >= s_idx
    eye = jnp.where(t_idx == s_idx, 1.0, 0.0)
    lane_r = lax.broadcasted_iota(jnp.int32, (LANES, LANES), 0) // HEAD
    lane_c = lax.broadcasted_iota(jnp.int32, (LANES, LANES), 1) // HEAD
    same_head = lane_r == lane_c

    def halves(x):
        zero = jnp.zeros_like(x)
        return jnp.where(left, x, zero), jnp.where(left, zero, x)

    def bdiag(x):
        x0, x1 = halves(x.astype(BF16))
        return jnp.concatenate([x0, x1], axis=0)

    def mm(a, rhs):
        return jnp.dot(a.astype(BF16), rhs, preferred_element_type=F32)


    def local_chain(start, p):
        rows = slice(start, start + c)
        sl = slice(p * LANES, (p + 1) * LANES)
        kq, rq = kq_s[rows, sl], rq_s[rows, sl]
        lhs = jnp.concatenate([kq, rq], axis=0)
        rhs = jnp.concatenate(halves(bk_s[rows, sl]) + halves(kx_s[rows, sl]), axis=0)
        aa = _dot_nt(lhs, rhs)
        yield
        a_ab = jnp.where(strict, aa[:c, :LANES], 0.0)
        a_ak = jnp.where(strict, aa[:c, LANES:], 0.0)
        a_rb = jnp.where(incl, aa[c:, :LANES], 0.0)
        a_rk = jnp.where(incl, aa[c:, LANES:], 0.0)
        yo = mm(jnp.concatenate([a_ak, a_rk], axis=0), bdiag(v_ref[rows, sl]))
        yield
        pw = -a_ab
        tinv = eye + pw
        pw = mm(pw, bdiag(pw))
        yield
        for _ in range(4):
            res = mm(pw, jnp.concatenate([bdiag(tinv), bdiag(pw)], axis=1))
            yield
            tinv = tinv + res[:, :LANES]
            pw = res[:, LANES:]
        tinv = tinv + mm(pw, bdiag(tinv))
        yield
        tw = mm(tinv, jnp.concatenate([bdiag(kq), bdiag(yo[:c])], axis=1))
        yield
        ar = mm(a_rb, jnp.concatenate([bdiag(tw[:, :LANES]), bdiag(tw[:, LANES:])], axis=1))
        yield
        tkq_s[rows, sl] = tw[:, :LANES].astype(BF16)
        w_s[rows, sl] = tw[:, LANES:]
        rqe_s[rows, sl] = (rq.astype(F32) - ar[:, :LANES]).astype(BF16)
        ol_s[rows, sl] = yo[c:] - ar[:, LANES:]

    def seq_group(gi):
        for u in range(group):
            start = (gi * group + u) * c
            rows = slice(start, start + c)
            states = [s_ref[p] for p in range(N_PAIRS)]
            xs = []
            for p in range(N_PAIRS):
                sl = slice(p * LANES, (p + 1) * LANES)
                lhs = jnp.concatenate([tkq_s[rows, sl], rqe_s[rows, sl]], axis=0)
                xs.append(_dot_nt(lhs, states[p].astype(BF16)))
            yield
            upds = []
            for p in range(N_PAIRS):
                sl = slice(p * LANES, (p + 1) * LANES)
                uu = xs[p][:c] + w_s[rows, sl]
                oo_s[rows, sl] = xs[p][c:] + ol_s[rows, sl]
                wv = jnp.concatenate([uu.astype(BF16), v_ref[rows, sl]], axis=0)
                wk = jnp.concatenate([nb_s[rows, sl], kd_s[rows, sl]], axis=0)
                upds.append(_dot_tn(wv, wk))
            yield
            for p in range(N_PAIRS):
                sl = slice(p * LANES, (p + 1) * LANES)
                s_ref[p] = states[p] * dec_s[start:start + 1, sl] + jnp.where(same_head, upds[p], 0.0)

    def local_group(gi):
        return [local_chain((gi * group + u) * c, p) for u in range(group) for p in range(N_PAIRS)]

    def post(gi):
        rows = slice(gi * gr, (gi + 1) * gr)
        e = e_ref[...]
        o = oo_s[rows, :]
        mean = _group_sum2(o, e) * (1.0 / HEAD)
        yield
        d = o - mean
        var = _group_sum2(d * d, e) * (1.0 / HEAD)
        yield
        v = v_ref[rows, :].astype(F32)
        bonus = _group_sum2(r_ref[rows, :].astype(F32) * k_ref[rows, :].astype(F32) * rk_ref[...], e)
        yield
        out = d * lax.rsqrt(var + RWKV_LN_EPS) * lnw_ref[...] + lnb_ref[...] + bonus * v
        o_ref[rows, :] = (out * g_ref[rows, :].astype(F32)).astype(o_ref.dtype)

    for step in range(n_groups + 4):
        chains = []
        if step < n_groups:
            chains.append(proj(step))
        if 0 <= step - 1 < n_groups:
            chains.append(prep(step - 1))
        if 0 <= step - 2 < n_groups:
            chains += local_group(step - 2)
        if 0 <= step - 3 < n_groups:
            chains.append(seq_group(step - 3))
        if 0 <= step - 4 < n_groups:
            chains.append(post(step - 4))
        yield from _lockstep_rounds(chains)


HGRN_TILE = 256

_RWKV_SRC = ("p", "ph", "mu", "w0", "a0", "wa2", "g2", "k_k", "k_a")
_RWKV_SRC_VRES = ("misc", "misch", "vmu", "v0", "v2", "vf")


def _mixers_kernel(*refs, has_vres):
    names = _RWKV_SRC + (_RWKV_SRC_VRES if has_vres else ())
    refs = list(refs)
    hgrn_in, refs = refs[:8], refs[8:]
    src, refs = dict(zip(names, refs[:len(names)])), refs[len(names):]
    (rk_ref, lnw_ref, lnb_ref, tri64_ref), refs = refs[:4], refs[4:]
    if has_vres:
        (oh_ref, or_ref), refs = refs[:2], refs[2:]
        vfo_ref = None
    else:
        (oh_ref, or_ref, vfo_ref), refs = refs[:3], refs[3:]
    hgrn_scratch, rwkv_scratch = refs[:6], refs[6:]
    e_ref = hgrn_in[6]

    @pl.when(pl.program_id(1) == 0)
    def _():
        hgrn_scratch[0][...] = jnp.zeros_like(hgrn_scratch[0])
        rwkv_scratch[7][...] = jnp.zeros_like(rwkv_scratch[7])

    def hgrn_all():
        for row0 in range(0, oh_ref.shape[0], HGRN_TILE):
            yield from _hgrn_stages(*hgrn_in, oh_ref, *hgrn_scratch, row0)

    rwkv_all = _rwkv_stages(src, rk_ref, lnw_ref, lnb_ref, tri64_ref, e_ref, or_ref, vfo_ref, *rwkv_scratch)
    hgrn = hgrn_all()
    n_h, n_r, tick, done_h = 3, 5, 0, False
    for _ in rwkv_all:
        if not done_h and ((tick + 1) * n_h) // n_r > (tick * n_h) // n_r:
            done_h = next(hgrn, "done") == "done"
        tick += 1
    for _ in hgrn:
        pass


def _mixers(main, aux, lower, o_gain, prm, v_first, r_k, ln_w, ln_b, bsz, seq, tm):
    nt = seq // tm
    has_vres = v_first is not None
    ch = COL_HGRN // MIX
    pw = 4 * MIX
    cp = COL_RWKV // pw
    halo = 16
    row = lambda cidx: (lambda b, i: (b * nt + i, cidx))
    const = lambda b, i: (0, 0)

    def halo_map(rows, cidx):
        per = tm // rows
        return lambda b, i: (jnp.maximum((b * nt + i) * per - 1, 0), cidx)

    spec = pl.BlockSpec((tm, MIX), row(0))
    vec = pl.BlockSpec((1, MIX), const)
    in_specs = [pl.BlockSpec((tm, MIX), row(ch)), pl.BlockSpec((tm, MIX), row(ch + 1)),
                pl.BlockSpec((tm, MIX), row(ch + 2)), spec, vec, vec,
                pl.BlockSpec((MIX // 2, MIX // 2), const),
                pl.BlockSpec((HGRN_TILE, HGRN_TILE), const),
                pl.BlockSpec((tm, pw), row(cp)),
                pl.BlockSpec((halo, pw), halo_map(halo, cp)),
                pl.BlockSpec((1, pw), const), vec, vec,
                pl.BlockSpec((LANES, 2 * MIX), const),
                pl.BlockSpec((LANES, MIX), const), vec, vec]
    args = [main, main, main, aux, lower, o_gain, _block_ones(MIX // 2), _block_tril(HGRN_TILE, HGRN_SUB),
            main, main, prm["mu"], prm["w0"], prm["a0"], prm["wa2"], prm["g2"], prm["k_k"], prm["k_a"]]
    if has_vres:
        in_specs += [pl.BlockSpec((tm, LANES), row(MIX // LANES)),
                     pl.BlockSpec((8, LANES), halo_map(8, MIX // LANES)),
                     pl.BlockSpec((1, LANES), const), vec,
                     pl.BlockSpec((LANES, MIX), const), spec]
        args += [aux, aux, prm["vmu"], prm["v0"], prm["v2"], v_first]
    in_specs += [vec, vec, vec, pl.BlockSpec((tm, tm), const)]
    args += [r_k, ln_w, ln_b, _block_tril(tm, RWKV_CHUNK)]
    h16 = pltpu.VMEM((HGRN_TILE, MIX), BF16)
    h32 = pltpu.VMEM((HGRN_TILE, MIX), F32)
    s16 = pltpu.VMEM((tm, MIX), BF16)
    s32 = pltpu.VMEM((tm, MIX), F32)
    state = pltpu.VMEM((N_PAIRS, LANES, LANES), F32)
    out = jax.ShapeDtypeStruct((bsz * seq, MIX), BF16)
    n_out = 2 if has_vres else 3
    res = pl.pallas_call(
        functools.partial(_mixers_kernel, has_vres=has_vres),
        grid=(bsz, nt),
        in_specs=in_specs,
        out_specs=[spec] * n_out,
        out_shape=[out] * n_out,
        scratch_shapes=[state, h16, h16, h16, h32, h32,
                        s16, s16, s16, s16, s16, s32, s16,
                        state, s16, s16, s16, s16, s16, s16, s32, s16, s16, s32, s32, s32],
        compiler_params=_cparams(("parallel", "arbitrary")),
        name="hgrn_rwkv",
    )(*args)
    return (res[0], res[1], v_first) if has_vres else tuple(res)


def _merge_kernel(yf_ref, yh_ref, yr_ref, gf_ref, gh_ref, gr_ref, x_ref, gt_ref, wb_ref, wo_ref, o_ref):
    def branch(y_ref, g_ref, i):
        return _sigmoid(g_ref[...].astype(F32)) * jnp.dot(y_ref[...], wb_ref[i], preferred_element_type=F32)

    merged = branch(yf_ref, gf_ref, 0) + branch(yh_ref, gh_ref, 1) + branch(yr_ref, gr_ref, 2)
    o_ref[...] = x_ref[...] + gt_ref[0] * jnp.dot(merged.astype(BF16), wo_ref[...], preferred_element_type=F32)


def _merge(y_fox, y_hgrn, y_rwkv, main, x, gt, w_branch, w_out, seq, tm):
    t, d = x.shape
    tps = seq // tm
    yspec = pl.BlockSpec((tm, MIX), lambda i: (i, 0))
    gspec = lambda cidx: pl.BlockSpec((tm, d), lambda i: (i, COL_GATE // d + cidx))
    return pl.pallas_call(
        _merge_kernel,
        grid=(t // tm,),
        in_specs=[yspec, yspec, yspec, gspec(0), gspec(1), gspec(2),
                  pl.BlockSpec((tm, d), lambda i: (i, 0)),
                  pl.BlockSpec((1, 1, d), lambda i: (i // tps, 0, 0)),
                  pl.BlockSpec((3, MIX, d), lambda i: (0, 0, 0)),
                  pl.BlockSpec((d, d), lambda i: (0, 0))],
        out_specs=pl.BlockSpec((tm, d), lambda i: (i, 0)),
        out_shape=jax.ShapeDtypeStruct((t, d), F32),
        compiler_params=_cparams(("parallel",)),
        name="merge_out",
    )(y_fox, y_hgrn, y_rwkv, main, main, main, x, gt, w_branch, w_out)


FFN_HALO = 16
FFN_TF = 256


def _ffn_body(x_ref, xh_ref, sc_ref, sh_ref, gt_ref, wup_ref, cw_ref, cb_ref, wd_ref, o_ref, h_ref, act_ref,
              *, tps):
    def normed(x):
        ms = jnp.mean(x * x, axis=-1, keepdims=True)
        return x * lax.rsqrt(ms + NORM_EPS) * sc_ref[0] + sh_ref[0]

    seq_start = pl.program_id(0) % tps == 0
    h_ref[0:FFN_HALO, :] = jnp.where(seq_start, 0.0, normed(xh_ref[...])).astype(BF16)
    h_ref[FFN_HALO:, :] = normed(x_ref[...]).astype(BF16)

    def up_proj(col):
        return jnp.dot(h_ref[...], wup_ref[:, col:col + FFN_TF], preferred_element_type=F32)

    def conv(u, col):
        cw = cw_ref[:, col:col + FFN_TF]
        y = cb_ref[:, col:col + FFN_TF] + cw[0:1] * pltpu.roll(u, 2, 0) + cw[1:2] * pltpu.roll(u, 1, 0) + cw[2:3] * u
        return y[FFN_HALO:]

    cols = [base + j * FFN_TF for j in range(D_FF // FFN_TF) for base in (0, D_FF)]
    u_next = up_proj(cols[0])
    val = None
    for i, col in enumerate(cols):
        u = u_next
        if i + 1 < len(cols):
            u_next = up_proj(cols[i + 1])
        y = conv(u, col)
        if i % 2 == 0:
            val = y
        else:
            act_ref[:, col - D_FF:col - D_FF + FFN_TF] = (y * _sigmoid(y) * val).astype(BF16)

    down = jnp.dot(act_ref[...], wd_ref[...], preferred_element_type=F32)
    o_ref[...] = x_ref[...] + gt_ref[0] * down


def _ffn(x, scale, shift, gt, w_up, conv_w, conv_b, w_down, seq, tm):
    t, d = x.shape
    tps = seq // tm
    per = tm // FFN_HALO
    bmap = lambda i: (i // tps, 0, 0)
    const = lambda i: (0, 0)
    resident = dict(pipeline_mode=pl.Buffered(1))
    return pl.pallas_call(
        functools.partial(_ffn_body, tps=tps),
        grid=(t // tm,),
        in_specs=[pl.BlockSpec((tm, d), lambda i: (i, 0)),
                  pl.BlockSpec((FFN_HALO, d), lambda i: (jnp.maximum(i * per - 1, 0), 0)),
                  pl.BlockSpec((1, 1, d), bmap),
                  pl.BlockSpec((1, 1, d), bmap),
                  pl.BlockSpec((1, 1, d), bmap),
                  pl.BlockSpec((d, 2 * D_FF), const, **resident),
                  pl.BlockSpec((3, 2 * D_FF), const),
                  pl.BlockSpec((1, 2 * D_FF), const),
                  pl.BlockSpec((D_FF, d), const, **resident)],
        out_specs=pl.BlockSpec((tm, d), lambda i: (i, 0)),
        out_shape=jax.ShapeDtypeStruct((t, d), F32),
        scratch_shapes=[pltpu.VMEM((tm + FFN_HALO, d), BF16), pltpu.VMEM((tm, D_FF), BF16)],
        compiler_params=_cparams(("parallel",)),
        name="conv_mlp",
    )(x, x, scale, shift, gt, w_up, conv_w, conv_b, w_down)


def _pack_in_proj(w_in, vres_down):
    depth, d, _ = w_in.shape
    fox, hgrn, rwkv, gate = 0, 1544, 3592, 5384
    w_in = w_in.astype(BF16)
    vres_down = vres_down.astype(BF16)
    z = lambda n: jnp.zeros((depth, d, n), w_in.dtype)
    main = jnp.concatenate([
        w_in[..., gate:gate + 3072],
        w_in[..., fox:fox + 1536],
        w_in[..., hgrn:hgrn + 512], w_in[..., hgrn + 1024:hgrn + 2048],
        w_in[..., rwkv:rwkv + 1792],
        z(N_MAIN - 7936)], axis=-1)
    vres = jnp.concatenate([jnp.zeros((1, d, VRES_RANK), w_in.dtype), vres_down], axis=0)
    aux = jnp.concatenate([
        w_in[..., hgrn + 512:hgrn + 1024],
        w_in[..., fox + 1536:fox + 1544], z(AUX_VRES_LANE - 8),
        vres, z(LANES - AUX_VRES_LANE - VRES_RANK)], axis=-1)
    assert main.shape[-1] == N_MAIN and aux.shape[-1] == N_AUX
    return main, aux


def _tile_heads(v):
    return jnp.tile(v, N_HEADS).reshape(1, MIX)


def _forward(x, c, w_ada, b_ada, norm1_g, norm2_g, w_in, fox_b_f, fox_q_gain, fox_k_gain,
             hgrn_lb, hgrn_o_gain, rwkv_mu, rwkv_w0, rwkv_w2, rwkv_a0, rwkv_a2, rwkv_g2,
             rwkv_k_k, rwkv_k_a, rwkv_r_k, rwkv_ln_w, rwkv_ln_b, rwkv_vres_down, rwkv_vres_mu,
             rwkv_v0, rwkv_v2, w_branch, w_out, w_up, conv_w, conv_b, w_down, *, tiles):
    bsz, seq, d = x.shape
    depth = w_in.shape[0]
    t = bsz * seq

    lb_prob = jax.nn.softmax(hgrn_lb.astype(F32), axis=0)
    hgrn_lower = jnp.cumsum(lb_prob, axis=0) - lb_prob[0]

    mod = _ada_mod(c, w_ada, b_ada)
    w_main, w_aux = _pack_in_proj(w_in, rwkv_vres_down)
    w_branch16, w_out16 = w_branch.astype(BF16), w_out.astype(BF16)
    w_up16, w_down16 = w_up.astype(BF16), w_down.astype(BF16)

    xf = x.reshape(t, d)
    v_first = None
    for l in range(depth):
        sh1, sc1, gt1, sh2, sc2, gt2 = [m.reshape(bsz, 1, d) for m in jnp.split(mod[l], 6, axis=-1)]
        scale1 = norm1_g[l] * (1.0 + sc1)
        b_f = jnp.zeros((1, LANES), F32).at[0, AUX_FF_LANE:AUX_FF_LANE + N_HEADS].set(fox_b_f[l])
        main, aux, qh, kh, vh = _in_proj(xf, scale1, sh1, w_main[l], w_aux[l],
                                         _tile_heads(fox_q_gain[l]) * HEAD ** -0.5, _tile_heads(fox_k_gain[l]), b_f,
                                         bsz, seq, tiles["proj_tm"], tiles["proj_tn"])
        y_fox = _fox_attn(qh, kh, vh, tiles["fox_tq"], tiles["fox_sub"], tiles["fox_tk"])

        wa2 = jnp.zeros((LANES, 2 * MIX), F32)
        wa2 = wa2.at[:HEAD, :MIX].set(rwkv_w2[l]).at[HEAD:, MIX:].set(rwkv_a2[l])
        prm = {
            "mu": jnp.concatenate([rwkv_mu[l], jnp.zeros((4 * MIX - rwkv_mu.shape[1],), F32)]).reshape(1, 4 * MIX),
            "w0": rwkv_w0[l].reshape(1, MIX), "a0": rwkv_a0[l].reshape(1, MIX),
            "wa2": wa2.astype(BF16), "g2": rwkv_g2[l].astype(BF16),
            "k_k": rwkv_k_k[l].reshape(1, MIX), "k_a": rwkv_k_a[l].reshape(1, MIX),
        }
        if l > 0:
            vmu = jnp.zeros((1, LANES), F32).at[0, AUX_VRES_LANE:AUX_VRES_LANE + VRES_RANK].set(rwkv_vres_mu[l - 1])
            v2 = jnp.zeros((LANES, MIX), F32).at[AUX_VRES_LANE:AUX_VRES_LANE + VRES_RANK].set(rwkv_v2[l - 1])
            prm.update(vmu=vmu, v0=rwkv_v0[l - 1].reshape(1, MIX), v2=v2.astype(BF16))
        y_hgrn, y_rwkv, v_first = _mixers(main, aux, hgrn_lower[l].reshape(1, MIX), _tile_heads(hgrn_o_gain[l]),
                                          prm, v_first, rwkv_r_k[l].reshape(1, MIX), rwkv_ln_w[l].reshape(1, MIX),
                                          rwkv_ln_b[l].reshape(1, MIX), bsz, seq, tiles["rwkv_tm"])

        xf = _merge(y_fox, y_hgrn, y_rwkv, main, xf, gt1, w_branch16[l], w_out16[l], seq, tiles["merge_tm"])

        scale2 = norm2_g[l] * (1.0 + sc2)
        xf = _ffn(xf, scale2, sh2, gt2, w_up16[l], conv_w[l], conv_b[l].reshape(1, -1), w_down16[l], seq,
                  tiles["ffn_tm"])
    return xf.reshape(bsz, seq, d)


def _tiles_for(seq):
    cap = lambda n: min(n, seq)
    return dict(proj_tm=cap(1024), proj_tn=1024, fox_tq=cap(1024), fox_sub=256, fox_tk=cap(1024),
                rwkv_tm=cap(512), merge_tm=cap(1024), ffn_tm=cap(1024))


def kernel(x, c, w_ada, b_ada, norm1_g, norm2_g, w_in, fox_b_f, fox_q_gain, fox_k_gain, hgrn_lb, hgrn_o_gain, rwkv_mu, rwkv_w0, rwkv_w2, rwkv_a0, rwkv_a2, rwkv_g2, rwkv_k_k, rwkv_k_a, rwkv_r_k, rwkv_ln_w, rwkv_ln_b, rwkv_vres_down, rwkv_vres_mu, rwkv_v0, rwkv_v2, w_branch, w_out, w_up, conv_w, conv_b, w_down):
    return _forward(x, c, w_ada, b_ada, norm1_g, norm2_g, w_in, fox_b_f, fox_q_gain, fox_k_gain,
                    hgrn_lb, hgrn_o_gain, rwkv_mu, rwkv_w0, rwkv_w2, rwkv_a0, rwkv_a2, rwkv_g2,
                    rwkv_k_k, rwkv_k_a, rwkv_r_k, rwkv_ln_w, rwkv_ln_b, rwkv_vres_down, rwkv_vres_mu,
                    rwkv_v0, rwkv_v2, w_branch, w_out, w_up, conv_w, conv_b, w_down,
                    tiles=_tiles_for(x.shape[1]))
```
